```python
import math
import jax, jax.numpy as jnp
from jax import lax
import numpy as np

D_MODEL = 1024
BATCH = 8
SEQ = 4096
DEPTH = 2

DA_HEADS = 4
DA_HEAD_DIM = 64
DA_WIDTH = DA_HEADS * 2 * DA_HEAD_DIM
CONV_CH = 512
CONV_WIDTH = 31
SB_HEADS = 8
SB_HEAD_DIM = 64
SB_WIDTH = SB_HEADS * SB_HEAD_DIM
N_BRANCH = 3
Q_BLOCK = 128
ROPE_THETA = 10000.0
MAX_POS_OFFSET = 1024
EPS = 1e-6
NEG_INF = -1e30
IN_SIZES = (DA_WIDTH, DA_WIDTH, DA_WIDTH, 2 * CONV_CH, SB_WIDTH, SB_WIDTH, SB_WIDTH, N_BRANCH * D_MODEL)
IN_COLS = sum(IN_SIZES)
IN_SPLITS = tuple(int(v) for v in np.cumsum(IN_SIZES)[:-1])
N_EXPERTS = 64
TOP_K = 8
N_GROUPS = 8
TOPK_GROUPS = 4
EXPERT_FF = 256
SHARED_FF = 256
ROUTED_SCALE = 2.5
EXPERT_BLOCK = 128

kernel_name = "hybrid_diffattn_conformer_stickbreak_moe"


def rms_norm(x, g):
    xf = x.astype(jnp.float32)
    y = xf * lax.rsqrt(jnp.mean(xf * xf, axis=-1, keepdims=True) + EPS)
    return (y * g.astype(jnp.float32)).astype(x.dtype)


def layer_norm(x, g, b):
    xf = x.astype(jnp.float32)
    mu = jnp.mean(xf, axis=-1, keepdims=True)
    var = jnp.mean(jnp.square(xf - mu), axis=-1, keepdims=True)
    y = (xf - mu) * lax.rsqrt(var + EPS) * g.astype(jnp.float32) + b.astype(jnp.float32)
    return y.astype(x.dtype)


def rope_tables(positions, dim):
    inv = ROPE_THETA ** (-jnp.arange(0, dim, 2, dtype=jnp.float32) / dim)
    ang = positions.astype(jnp.float32)[..., None] * inv
    return jnp.cos(ang)[:, :, None, :], jnp.sin(ang)[:, :, None, :]


def apply_rope(x, cos, sin):
    x1, x2 = jnp.split(x, 2, axis=-1)
    c = cos.astype(x.dtype)
    s = sin.astype(x.dtype)
    return jnp.concatenate([x1 * c - x2 * s, x2 * c + x1 * s], axis=-1)


def to_blocks(x):
    b, s = x.shape[:2]
    x = x.reshape((b, s // Q_BLOCK, Q_BLOCK) + x.shape[2:])
    return jnp.moveaxis(x, 1, 0)


def from_blocks(y):
    y = jnp.moveaxis(y, 0, 1)
    return y.reshape((y.shape[0], y.shape[1] * y.shape[2]) + y.shape[3:])


def diff_attention(q, k, v, lam, lambda_init, subln_g):
    S = k.shape[1]
    n_blocks = S // Q_BLOCK
    scale = DA_HEAD_DIM ** -0.5
    kpos = jnp.arange(S)

    def block(args):
        qb, start = args
        s = jnp.einsum('bqhmd,bkhmd->bhmqk', qb, k).astype(jnp.float32) * scale
        qpos = start + jnp.arange(Q_BLOCK)
        causal = kpos[None, :] <= qpos[:, None]
        p = jax.nn.softmax(jnp.where(causal, s, NEG_INF), axis=-1)
        a = p[:, :, 0] - lam * p[:, :, 1]
        return jnp.einsum('bhqk,bkhe->bqhe', a.astype(v.dtype), v)

    starts = jnp.arange(n_blocks, dtype=jnp.int32) * Q_BLOCK
    o = from_blocks(lax.map(block, (to_blocks(q), starts)))
    o = rms_norm(o, subln_g) * (1.0 - lambda_init)
    return o.reshape(o.shape[0], S, DA_WIDTH)


def stick_breaking(q, k, v):
    S = k.shape[1]
    n_blocks = S // Q_BLOCK
    scale = SB_HEAD_DIM ** -0.5
    kpos = jnp.arange(S)

    def block(args):
        qb, start = args
        z = jnp.einsum('bqhd,bkhd->bhqk', qb, k).astype(jnp.float32) * scale
        qpos = start + jnp.arange(Q_BLOCK)
        before = kpos[None, :] < qpos[:, None]
        log_beta = jax.nn.log_sigmoid(z)
        log_one_minus = jnp.where(before, jax.nn.log_sigmoid(-z), 0.0)
        key_axis = log_one_minus.ndim - 1
        suffix = lax.cumsum(log_one_minus, axis=key_axis, reverse=True) - log_one_minus
        a = jnp.where(before, jnp.exp(log_beta + suffix), 0.0)
        return jnp.einsum('bhqk,bkhd->bqhd', a.astype(v.dtype), v)

    starts = jnp.arange(n_blocks, dtype=jnp.int32) * Q_BLOCK
    o = from_blocks(lax.map(block, (to_blocks(q), starts)))
    return o.reshape(o.shape[0], S, SB_WIDTH)


def conformer_conv(u, w_dw, b_dw, ln_g, ln_b):
    a, g = jnp.split(u, 2, axis=-1)
    h = a * jax.nn.sigmoid(g)
    h = lax.conv_general_dilated(
        h, w_dw[:, None, :].astype(h.dtype), window_strides=(1,),
        padding=[(CONV_WIDTH - 1, 0)], dimension_numbers=('NWC', 'WIO', 'NWC'),
        feature_group_count=CONV_CH) + b_dw
    return jax.nn.silu(layer_norm(h, ln_g, ln_b))


def mixer(h, cos, sin, lambda_init, w_in, qn_g, kn_g, lam_q1, lam_k1, lam_q2, lam_k2, subln_g,
          w_proj_a, w_dw, b_dw, conv_ln_g, conv_ln_b, w_proj_b, b_proj_b, w_proj_c, w_out):
    B, S, _ = h.shape
    proj = h @ w_in
    qa, ka, va, ub, qc, kc, vc, gates = jnp.split(proj, IN_SPLITS, axis=-1)

    qa = apply_rope(rms_norm(qa.reshape(B, S, 2 * DA_HEADS, DA_HEAD_DIM), qn_g), cos, sin)
    ka = apply_rope(rms_norm(ka.reshape(B, S, 2 * DA_HEADS, DA_HEAD_DIM), kn_g), cos, sin)
    qa = qa.reshape(B, S, DA_HEADS, 2, DA_HEAD_DIM)
    ka = ka.reshape(B, S, DA_HEADS, 2, DA_HEAD_DIM)
    va = va.reshape(B, S, DA_HEADS, 2 * DA_HEAD_DIM)
    lam = (jnp.exp(jnp.sum(lam_q1.astype(jnp.float32) * lam_k1.astype(jnp.float32)))
           - jnp.exp(jnp.sum(lam_q2.astype(jnp.float32) * lam_k2.astype(jnp.float32)))
           + lambda_init)
    y_a = diff_attention(qa, ka, va, lam, lambda_init, subln_g) @ w_proj_a

    y_b = conformer_conv(ub, w_dw, b_dw, conv_ln_g, conv_ln_b) @ w_proj_b + b_proj_b

    y_c = stick_breaking(qc.reshape(B, S, SB_HEADS, SB_HEAD_DIM),
                         kc.reshape(B, S, SB_HEADS, SB_HEAD_DIM),
                         vc.reshape(B, S, SB_HEADS, SB_HEAD_DIM)) @ w_proj_c

    g = jax.nn.sigmoid(gates.reshape(B, S, N_BRANCH, D_MODEL))
    merged = g[:, :, 0] * y_a + g[:, :, 1] * y_b + g[:, :, 2] * y_c
    return merged @ w_out


def swiglu(x, w1, w3, w2):
    return (jax.nn.silu(x @ w1) * (x @ w3)) @ w2


def moe_ffn(u, w_router, b_router, w1, w3, w2, ws1, ws3, ws2):
    B, S, D = u.shape
    T = B * S
    xf = u.reshape(T, D)
    scores = jax.nn.sigmoid((xf @ w_router).astype(jnp.float32))
    biased = scores + b_router.astype(jnp.float32)
    grouped = biased.reshape(T, N_GROUPS, N_EXPERTS // N_GROUPS)
    group_score = jnp.sum(lax.top_k(grouped, 2)[0], axis=-1)
    _, top_groups = lax.top_k(group_score, TOPK_GROUPS)
    group_mask = jnp.any(top_groups[..., None] == jnp.arange(N_GROUPS), axis=-2)
    expert_mask = jnp.repeat(group_mask, N_EXPERTS // N_GROUPS, axis=-1)
    _, idx = lax.top_k(jnp.where(expert_mask, biased, NEG_INF), TOP_K)
    w = jnp.take_along_axis(scores, idx, axis=-1)
    w = w / (jnp.sum(w, axis=-1, keepdims=True) + 1e-20) * ROUTED_SCALE

    TK = T * TOP_K
    flat_e = idx.reshape(TK).astype(jnp.int32)
    flat_tok = jnp.arange(TK, dtype=jnp.int32) // TOP_K
    flat_w = w.reshape(TK)
    order = jnp.argsort(flat_e)
    se, st, sw = flat_e[order], flat_tok[order], flat_w[order]
    counts = jnp.bincount(flat_e, length=N_EXPERTS).astype(jnp.int32)
    starts = jnp.cumsum(counts) - counts
    padded = (counts + EXPERT_BLOCK - 1) // EXPERT_BLOCK * EXPERT_BLOCK
    pad_ends = jnp.cumsum(padded)
    pad_starts = pad_ends - padded
    dest = pad_starts[se] + (jnp.arange(TK, dtype=jnp.int32) - starts[se])
    n_blocks = (TK + N_EXPERTS * (EXPERT_BLOCK - 1) + EXPERT_BLOCK - 1) // EXPERT_BLOCK
    n_slots = n_blocks * EXPERT_BLOCK
    slot_tok = jnp.full((n_slots,), T, jnp.int32).at[dest].set(st)
    slot_w = jnp.zeros((n_slots,), jnp.float32).at[dest].set(sw)
    block_e = jnp.minimum(jnp.searchsorted(pad_ends, jnp.arange(n_blocks, dtype=jnp.int32) * EXPERT_BLOCK,
                                           side='right'), N_EXPERTS - 1).astype(jnp.int32)
    xpad = jnp.concatenate([xf, jnp.zeros((1, D), xf.dtype)], axis=0)

    def expert_block(args):
        toks, e, wts = args
        out = swiglu(xpad[toks], w1[e], w3[e], w2[e])
        return out * wts[:, None].astype(out.dtype)

    y = lax.map(expert_block, (slot_tok.reshape(n_blocks, EXPERT_BLOCK), block_e,
                               slot_w.reshape(n_blocks, EXPERT_BLOCK)))
    routed = jax.ops.segment_sum(y.reshape(n_slots, D), slot_tok, num_segments=T + 1)[:T]
    shared = swiglu(xf, ws1, ws3, ws2)
    return (routed + shared).reshape(B, S, D)


def setup_inputs(seed: int = 0) -> dict:
    key = jax.random.key(seed)
    ks = iter(jax.random.split(key, 40))
    L = DEPTH

    def nrm(shape, s):
        return jax.random.normal(next(ks), shape, jnp.float32) * s

    def gain(shape):
        return 1.0 + nrm(shape, 0.02)

    D = D_MODEL
    x = nrm((BATCH, SEQ, D), 1.0)
    c = nrm((BATCH, D), 1.0)
    positions = (jnp.arange(SEQ, dtype=jnp.int32)[None, :]
                 + jax.random.randint(next(ks), (BATCH, 1), 0, MAX_POS_OFFSET, dtype=jnp.int32))
    return {
        "x": x, "c": c, "positions": positions,
        "w_mod": nrm((L, D, 6 * D), 0.3 * D ** -0.5),
        "b_mod": nrm((L, 6 * D), 0.02),
        "norm_mix_g": gain((L, D)),
        "norm_ffn_g": gain((L, D)),
        "w_in": nrm((L, D, IN_COLS), D ** -0.5),
        "qn_g": gain((L, DA_HEAD_DIM)),
        "kn_g": gain((L, DA_HEAD_DIM)),
        "lam_q1": nrm((L, DA_HEAD_DIM), 0.1),
        "lam_k1": nrm((L, DA_HEAD_DIM), 0.1),
        "lam_q2": nrm((L, DA_HEAD_DIM), 0.1),
        "lam_k2": nrm((L, DA_HEAD_DIM), 0.1),
        "subln_g": gain((L, 2 * DA_HEAD_DIM)),
        "w_proj_a": nrm((L, DA_WIDTH, D), DA_WIDTH ** -0.5),
        "w_dw": nrm((L, CONV_WIDTH, CONV_CH), CONV_WIDTH ** -0.5),
        "b_dw": nrm((L, CONV_CH), 0.02),
        "conv_ln_g": gain((L, CONV_CH)),
        "conv_ln_b": nrm((L, CONV_CH), 0.02),
        "w_proj_b": nrm((L, CONV_CH, D), CONV_CH ** -0.5),
        "b_proj_b": nrm((L, D), 0.02),
        "w_proj_c": nrm((L, SB_WIDTH, D), SB_WIDTH ** -0.5),
        "w_out": nrm((L, D, D), D ** -0.5),
        "w_router": nrm((L, D, N_EXPERTS), D ** -0.5),
        "b_router": nrm((L, N_EXPERTS), 0.01),
        "w1": nrm((L, N_EXPERTS, D, EXPERT_FF), D ** -0.5),
        "w3": nrm((L, N_EXPERTS, D, EXPERT_FF), D ** -0.5),
        "w2": nrm((L, N_EXPERTS, EXPERT_FF, D), EXPERT_FF ** -0.5),
        "ws1": nrm((L, D, SHARED_FF), D ** -0.5),
        "ws3": nrm((L, D, SHARED_FF), D ** -0.5),
        "ws2": nrm((L, SHARED_FF, D), SHARED_FF ** -0.5),
    }


def reference(x, c, positions, w_mod, b_mod, norm_mix_g, norm_ffn_g, w_in, qn_g, kn_g,
              lam_q1, lam_k1, lam_q2, lam_k2, subln_g, w_proj_a, w_dw, b_dw, conv_ln_g, conv_ln_b,
              w_proj_b, b_proj_b, w_proj_c, w_out, w_router, b_router, w1, w3, w2, ws1, ws3, ws2):
    cos, sin = rope_tables(positions, DA_HEAD_DIM)
    c_act = jax.nn.silu(c)
    for l in range(DEPTH):
        lambda_init = 0.8 - 0.6 * math.exp(-0.3 * l)
        mod = c_act @ w_mod[l] + b_mod[l]
        sh_m, sc_m, g_m, sh_f, sc_f, g_f = [m[:, None, :] for m in jnp.split(mod, 6, axis=-1)]
        h = rms_norm(x, norm_mix_g[l]) * (1.0 + sc_m) + sh_m
        x = x + g_m * mixer(h, cos, sin, lambda_init, w_in[l], qn_g[l], kn_g[l], lam_q1[l], lam_k1[l],
                            lam_q2[l], lam_k2[l], subln_g[l], w_proj_a[l], w_dw[l], b_dw[l],
                            conv_ln_g[l], conv_ln_b[l], w_proj_b[l], b_proj_b[l], w_proj_c[l], w_out[l])
        h = rms_norm(x, norm_ffn_g[l]) * (1.0 + sc_f) + sh_f
        x = x + g_f * moe_ffn(h, w_router[l], b_router[l], w1[l], w3[l], w2[l], ws1[l], ws3[l], ws2[l])
    return x
```

```python
import functools
import math

import jax
import jax.numpy as jnp
from jax import lax
from jax.experimental import pallas as pl
from jax.experimental.pallas import tpu as pltpu

F32 = jnp.float32
BF16 = jnp.bfloat16

D_MODEL = 1024
DA_HEADS = 4
DA_HEAD_DIM = 64
DA_WIDTH = 512
CONV_CH = 512
CONV_WIDTH = 31
SB_WIDTH = 512
SB_HEAD_DIM = 64
IN_COLS = 7168
ROPE_THETA = 10000.0
EPS = 1e-6
NEG_INF = -1e30
N_EXPERTS = 64
TOP_K = 8
N_GROUPS = 8
TOPK_GROUPS = 4
GROUP_SIZE = N_EXPERTS // N_GROUPS
EXPERT_FF = 256
ROUTED_SCALE = 2.5

LANES = 128
CONV_HALO = 32

COL_QA, COL_KA, COL_VA, COL_UB, COL_QC, COL_KC, COL_VC, COL_GATES = 0, 512, 1024, 1536, 2560, 3072, 3584, 4096

VMEM_LIMIT = 48 * 1024 * 1024


def _cparams(sem):
    return pltpu.CompilerParams(dimension_semantics=sem, vmem_limit_bytes=VMEM_LIMIT)


def _dot(a, b):
    return jnp.dot(a, b, preferred_element_type=F32)


def _dot_nt(a, b):
    return lax.dot_general(a, b, (((1,), (1,)), ((), ())), preferred_element_type=F32)


def _split_hi_lo(x):
    hi = x.astype(BF16)
    lo = (x - hi.astype(F32)).astype(BF16)
    return hi, lo


def _mod_kernel(c_ref, w_ref, b_ref, o_ref):
    c = c_ref[...]
    c_act = c * jax.nn.sigmoid(c)
    o_ref[0] = jnp.dot(c_act, w_ref[0], precision=lax.Precision.HIGHEST,
                       preferred_element_type=F32) + b_ref[0]


def _modulation(c, w_mod, b_mod):
    L, D, N = w_mod.shape
    B = c.shape[0]
    tn = 1536
    return pl.pallas_call(
        _mod_kernel,
        grid=(L, N // tn),
        in_specs=[pl.BlockSpec((B, D), lambda l, j: (0, 0)),
                  pl.BlockSpec((1, D, tn), lambda l, j: (l, 0, j)),
                  pl.BlockSpec((1, 1, tn), lambda l, j: (l, 0, j))],
        out_specs=pl.BlockSpec((1, B, tn), lambda l, j: (l, 0, j)),
        out_shape=jax.ShapeDtypeStruct((L, B, N), F32),
        compiler_params=_cparams(("arbitrary", "arbitrary")),
        name="modulation",
    )(c, w_mod, b_mod.reshape(L, 1, N))


def _rope_kernel(pos_ref, inv_ref, cos_ref, sin_ref):
    ang = pos_ref[...] * inv_ref[...]
    lane = lax.broadcasted_iota(jnp.int32, ang.shape, 1)
    first_half = (lane % DA_HEAD_DIM) < (DA_HEAD_DIM // 2)
    cos_ref[...] = jnp.cos(ang)
    s = jnp.sin(ang)
    sin_ref[...] = jnp.where(first_half, -s, s)


def _rope_tables(pos_lanes, inv_lanes):
    T = pos_lanes.shape[0]
    tm = 2048
    return pl.pallas_call(
        _rope_kernel,
        grid=(T // tm,),
        in_specs=[pl.BlockSpec((tm, LANES), lambda i: (i, 0)),
                  pl.BlockSpec((1, LANES), lambda i: (0, 0))],
        out_specs=[pl.BlockSpec((tm, LANES), lambda i: (i, 0))] * 2,
        out_shape=[jax.ShapeDtypeStruct((T, LANES), F32)] * 2,
        compiler_params=_cparams(("arbitrary",)),
        name="rope_tables",
    )(pos_lanes, inv_lanes)


def _ada_rms(x, g, sc, sh):
    ms = jnp.mean(x * x, axis=-1, keepdims=True)
    return x * lax.rsqrt(ms + EPS) * g * (1.0 + sc) + sh


def _norm_proj_kernel(x_ref, g_ref, sc_ref, sh_ref, w_ref, o_ref, h_scr):
    @pl.when(pl.program_id(1) == 0)
    def _():
        h_scr[...] = _ada_rms(x_ref[...], g_ref[...], sc_ref[0], sh_ref[0]).astype(BF16)

    o_ref[...] = _dot(h_scr[...], w_ref[...]).astype(BF16)


def _norm_proj(x, g, sc, sh, w_bf16, S):
    T, D = x.shape
    N = w_bf16.shape[1]
    tm, tn = 1024, 1024
    per_batch = S // tm
    return pl.pallas_call(
        _norm_proj_kernel,
        grid=(T // tm, N // tn),
        in_specs=[pl.BlockSpec((tm, D), lambda i, j: (i, 0)),
                  pl.BlockSpec((1, D), lambda i, j: (0, 0)),
                  pl.BlockSpec((1, 1, D), lambda i, j: (i // per_batch, 0, 0)),
                  pl.BlockSpec((1, 1, D), lambda i, j: (i // per_batch, 0, 0)),
                  pl.BlockSpec((D, tn), lambda i, j: (0, j))],
        out_specs=pl.BlockSpec((tm, tn), lambda i, j: (i, j)),
        out_shape=jax.ShapeDtypeStruct((T, N), BF16),
        scratch_shapes=[pltpu.VMEM((tm, D), BF16)],
        compiler_params=_cparams(("arbitrary", "arbitrary")),
        name="norm_proj",
    )(x, g, sc, sh, w_bf16)


def _qk_prep_kernel(q_ref, k_ref, cos_ref, sin_ref, bd_ref, qg_ref, kg_ref, qo_ref, ko_ref):
    cos = cos_ref[...]
    sin = sin_ref[...]
    bd = bd_ref[...]
    lane = lax.broadcasted_iota(jnp.int32, cos.shape, 1)
    first_half = (lane % DA_HEAD_DIM) < (DA_HEAD_DIM // 2)
    half = DA_HEAD_DIM // 2

    def prep(x_ref, g_ref, o_ref, scale):
        x = x_ref[...].astype(F32)
        hi, lo = _split_hi_lo(x * x)
        ms = _dot(hi, bd) + _dot(lo, bd)
        y = x * lax.rsqrt(ms + EPS) * g_ref[...]
        for c in range(DA_WIDTH // LANES):
            yc = y[:, c * LANES:(c + 1) * LANES]
            swapped = jnp.where(first_half, pltpu.roll(yc, LANES - half, 1), pltpu.roll(yc, half, 1))
            o_ref[:, c * LANES:(c + 1) * LANES] = ((yc * cos + swapped * sin) * scale).astype(BF16)

    prep(q_ref, qg_ref, qo_ref, DA_HEAD_DIM ** -0.5)
    prep(k_ref, kg_ref, ko_ref, 1.0)


def _qk_prep(proj, cos, sin, bd, qg, kg):
    T = proj.shape[0]
    tm = 1024
    W = DA_WIDTH
    return pl.pallas_call(
        _qk_prep_kernel,
        grid=(T // tm,),
        in_specs=[pl.BlockSpec((tm, W), lambda i: (i, COL_QA // W)),
                  pl.BlockSpec((tm, W), lambda i: (i, COL_KA // W)),
                  pl.BlockSpec((tm, LANES), lambda i: (i, 0)),
                  pl.BlockSpec((tm, LANES), lambda i: (i, 0)),
                  pl.BlockSpec((W, W), lambda i: (0, 0)),
                  pl.BlockSpec((1, W), lambda i: (0, 0)),
                  pl.BlockSpec((1, W), lambda i: (0, 0))],
        out_specs=[pl.BlockSpec((tm, W), lambda i: (i, 0))] * 2,
        out_shape=[jax.ShapeDtypeStruct((T, W), BF16)] * 2,
        compiler_params=_cparams(("arbitrary",)),
        name="qk_prep",
    )(proj, proj, cos, sin, bd, qg, kg)


def _diff_attn_kernel(lam_ref, q_ref, k_ref, v_ref, g_ref, o_ref, acc1, acc2, *, tq, out_scale):
    qi = pl.program_id(2)
    qf = q_ref[...].astype(F32)
    lane = lax.broadcasted_iota(jnp.int32, qf.shape, 1)
    q1 = jnp.where(lane < DA_HEAD_DIM, qf, 0.0).astype(BF16)
    q2 = jnp.where(lane >= DA_HEAD_DIM, qf, 0.0).astype(BF16)
    acc1[...] = jnp.zeros_like(acc1)
    acc2[...] = jnp.zeros_like(acc2)
    row = lax.broadcasted_iota(jnp.int32, (tq, tq), 0)
    col = lax.broadcasted_iota(jnp.int32, (tq, tq), 1)

    def one_map(qm, k, v, m, l, acc, masked):
        s = _dot_nt(qm, k)
        if masked:
            s = jnp.where(col <= row, s, NEG_INF)
        m_new = jnp.maximum(m, jnp.max(s, axis=1, keepdims=True))
        p = jnp.exp(s - m_new)
        alpha = jnp.exp(m - m_new)
        acc[...] = alpha * acc[...] + _dot(p.astype(BF16), v)
        return m_new, alpha * l + jnp.sum(p, axis=1, keepdims=True)

    def step(kj, carry, masked):
        m1, l1, m2, l2 = carry
        start = pl.multiple_of(kj * tq, tq)
        k = k_ref[pl.ds(start, tq), :]
        v = v_ref[pl.ds(start, tq), :]
        m1, l1 = one_map(q1, k, v, m1, l1, acc1, masked)
        m2, l2 = one_map(q2, k, v, m2, l2, acc2, masked)
        return m1, l1, m2, l2

    neg = jnp.full((tq, 1), NEG_INF, F32)
    zero = jnp.zeros((tq, 1), F32)
    carry = lax.fori_loop(0, qi, lambda kj, c: step(kj, c, False), (neg, zero, neg, zero))
    _, l1, _, l2 = step(qi, carry, True)
    o = acc1[...] / l1 - lam_ref[...] * (acc2[...] / l2)
    ms = jnp.mean(o * o, axis=-1, keepdims=True)
    o_ref[...] = (o * lax.rsqrt(ms + EPS) * g_ref[...] * out_scale).astype(BF16)


def _diff_attn(lam_lanes, q_r, k_r, proj, subln_g, B, S, out_scale):
    T = q_r.shape[0]
    tq = 512
    nq = S // tq
    return pl.pallas_call(
        functools.partial(_diff_attn_kernel, tq=tq, out_scale=out_scale),
        grid=(B, DA_HEADS, nq),
        in_specs=[pl.BlockSpec((1, LANES), lambda b, h, i: (0, 0)),
                  pl.BlockSpec((tq, LANES), lambda b, h, i: (b * nq + i, h)),
                  pl.BlockSpec((S, LANES), lambda b, h, i: (b, h)),
                  pl.BlockSpec((S, LANES), lambda b, h, i: (b, COL_VA // LANES + h)),
                  pl.BlockSpec((1, LANES), lambda b, h, i: (0, 0))],
        out_specs=pl.BlockSpec((tq, LANES), lambda b, h, i: (b * nq + i, h)),
        out_shape=jax.ShapeDtypeStruct((T, DA_WIDTH), BF16),
        scratch_shapes=[pltpu.VMEM((tq, LANES), F32), pltpu.VMEM((tq, LANES), F32)],
        compiler_params=_cparams(("arbitrary", "arbitrary", "arbitrary")),
        name="diff_attn",
    )(lam_lanes, q_r, k_r, proj, subln_g)


def _sb_attn_kernel(q_ref, k_ref, v_ref, u_ref, o_ref, acc1, acc2, *, tq):
    qi = pl.program_id(2)
    qf = q_ref[...].astype(F32) * (SB_HEAD_DIM ** -0.5)
    lane = lax.broadcasted_iota(jnp.int32, qf.shape, 1)
    q1 = jnp.where(lane < SB_HEAD_DIM, qf, 0.0).astype(BF16)
    q2 = jnp.where(lane >= SB_HEAD_DIM, qf, 0.0).astype(BF16)
    u = u_ref[...]
    acc1[...] = jnp.zeros_like(acc1)
    acc2[...] = jnp.zeros_like(acc2)
    row = lax.broadcasted_iota(jnp.int32, (tq, tq), 0)
    col = lax.broadcasted_iota(jnp.int32, (tq, tq), 1)
    before = col < row

    def one_head(qm, k, v, r, acc, masked):
        z = _dot_nt(qm, k)
        log_beta = jnp.minimum(z, 0.0) - jnp.log(1.0 + jnp.exp(-jnp.abs(z)))
        log_1m = log_beta - z
        if masked:
            log_1m = jnp.where(before, log_1m, 0.0)
        hi, lo = _split_hi_lo(log_1m)
        suffix = _dot(hi, u) + _dot(lo, u)
        a = jnp.exp(log_beta + suffix + r)
        if masked:
            a = jnp.where(before, a, 0.0)
        acc[...] += _dot(a.astype(BF16), v)
        return r + jnp.sum(log_1m, axis=1, keepdims=True)

    def step(kj, carry, masked):
        r1, r2 = carry
        start = pl.multiple_of(kj * tq, tq)
        k = k_ref[pl.ds(start, tq), :]
        v = v_ref[pl.ds(start, tq), :]
        return one_head(q1, k, v, r1, acc1, masked), one_head(q2, k, v, r2, acc2, masked)

    zero = jnp.zeros((tq, 1), F32)
    carry = step(qi, (zero, zero), True)
    lax.fori_loop(0, qi, lambda t, c: step(qi - 1 - t, c, False), carry)
    o_ref[...] = jnp.where(lane < SB_HEAD_DIM, acc1[...], acc2[...]).astype(BF16)


def _sb_attn(proj, u, B, S):
    T = proj.shape[0]
    tq = 256
    nq = S // tq
    return pl.pallas_call(
        functools.partial(_sb_attn_kernel, tq=tq),
        grid=(B, SB_WIDTH // LANES, nq),
        in_specs=[pl.BlockSpec((tq, LANES), lambda b, p, i: (b * nq + i, COL_QC // LANES + p)),
                  pl.BlockSpec((S, LANES), lambda b, p, i: (b, COL_KC // LANES + p)),
                  pl.BlockSpec((S, LANES), lambda b, p, i: (b, COL_VC // LANES + p)),
                  pl.BlockSpec((tq, tq), lambda b, p, i: (0, 0))],
        out_specs=pl.BlockSpec((tq, LANES), lambda b, p, i: (b * nq + i, p)),
        out_shape=jax.ShapeDtypeStruct((T, SB_WIDTH), BF16),
        scratch_shapes=[pltpu.VMEM((tq, LANES), F32), pltpu.VMEM((tq, LANES), F32)],
        compiler_params=_cparams(("arbitrary", "arbitrary", "arbitrary")),
        name="sb_attn",
    )(proj, proj, proj, u)


def _conv_kernel(a_ref, g_ref, ah_ref, gh_ref, w_ref, b_ref, lg_ref, lb_ref, o_ref, hbuf, *, tm, per_batch, rows):
    i = pl.program_id(0)
    halo = ah_ref[...].astype(F32) * jax.nn.sigmoid(gh_ref[...].astype(F32))
    hbuf[0:CONV_HALO, :] = jnp.where(i % per_batch == 0, 0.0, halo)
    hbuf[CONV_HALO:, :] = a_ref[...].astype(F32) * jax.nn.sigmoid(g_ref[...].astype(F32))
    w = w_ref[...]
    first = CONV_HALO - (CONV_WIDTH - 1)
    for r0 in range(0, tm, rows):
        acc = jnp.broadcast_to(b_ref[...], (rows, CONV_CH))
        for t in range(CONV_WIDTH):
            acc = acc + hbuf[r0 + first + t:r0 + first + t + rows, :] * w[t:t + 1, :]
        mu = jnp.mean(acc, axis=-1, keepdims=True)
        d = acc - mu
        var = jnp.mean(d * d, axis=-1, keepdims=True)
        y = d * lax.rsqrt(var + EPS) * lg_ref[...] + lb_ref[...]
        o_ref[r0:r0 + rows, :] = (y * jax.nn.sigmoid(y)).astype(BF16)


def _conv_module(proj, w_dw, b_dw, ln_g, ln_b, S):
    T = proj.shape[0]
    tm = 512
    per_batch = S // tm
    C = CONV_CH
    ca, cg = COL_UB // C, COL_UB // C + 1
    halo_blocks = tm // CONV_HALO

    def halo_idx(col):
        return lambda i: (jnp.maximum(i * halo_blocks - 1, 0), col)

    return pl.pallas_call(
        functools.partial(_conv_kernel, tm=tm, per_batch=per_batch, rows=64),
        grid=(T // tm,),
        in_specs=[pl.BlockSpec((tm, C), lambda i: (i, ca)),
                  pl.BlockSpec((tm, C), lambda i: (i, cg)),
                  pl.BlockSpec((CONV_HALO, C), halo_idx(ca)),
                  pl.BlockSpec((CONV_HALO, C), halo_idx(cg)),
                  pl.BlockSpec((CONV_WIDTH, C), lambda i: (0, 0)),
                  pl.BlockSpec((1, C), lambda i: (0, 0)),
                  pl.BlockSpec((1, C), lambda i: (0, 0)),
                  pl.BlockSpec((1, C), lambda i: (0, 0))],
        out_specs=pl.BlockSpec((tm, C), lambda i: (i, 0)),
        out_shape=jax.ShapeDtypeStruct((T, C), BF16),
        scratch_shapes=[pltpu.VMEM((tm + CONV_HALO, C), F32)],
        compiler_params=_cparams(("arbitrary",)),
        name="conv_module",
    )(proj, proj, proj, proj, w_dw, b_dw, ln_g, ln_b)


def _merge_kernel(oa_ref, cb_ref, oc_ref, g0_ref, g1_ref, g2_ref, x_ref, gm_ref,
                  wa_ref, wb_ref, bb_ref, wc_ref, wo_ref, o_ref):
    y_a = _dot(oa_ref[...], wa_ref[...])
    y_b = _dot(cb_ref[...], wb_ref[...]) + bb_ref[...]
    y_c = _dot(oc_ref[...], wc_ref[...])
    merged = (jax.nn.sigmoid(g0_ref[...].astype(F32)) * y_a
              + jax.nn.sigmoid(g1_ref[...].astype(F32)) * y_b
              + jax.nn.sigmoid(g2_ref[...].astype(F32)) * y_c)
    o_ref[...] = x_ref[...] + gm_ref[0] * _dot(merged.astype(BF16), wo_ref[...])


def _merge(o_a, cb, o_c, proj, x, g_m, wa, wb, bb, wc, wo, S):
    T, D = x.shape
    tm = 512
    per_batch = S // tm
    W = DA_WIDTH
    gcol = COL_GATES // D
    branch = pl.BlockSpec((tm, W), lambda i: (i, 0))
    wspec = pl.BlockSpec((W, D), lambda i: (0, 0))
    return pl.pallas_call(
        _merge_kernel,
        grid=(T // tm,),
        in_specs=[branch, branch, branch,
                  pl.BlockSpec((tm, D), lambda i: (i, gcol)),
                  pl.BlockSpec((tm, D), lambda i: (i, gcol + 1)),
                  pl.BlockSpec((tm, D), lambda i: (i, gcol + 2)),
                  pl.BlockSpec((tm, D), lambda i: (i, 0)),
                  pl.BlockSpec((1, 1, D), lambda i: (i // per_batch, 0, 0)),
                  wspec, wspec, pl.BlockSpec((1, D), lambda i: (0, 0)), wspec,
                  pl.BlockSpec((D, D), lambda i: (0, 0))],
        out_specs=pl.BlockSpec((tm, D), lambda i: (i, 0)),
        out_shape=jax.ShapeDtypeStruct((T, D), F32),
        compiler_params=_cparams(("arbitrary",)),
        name="merge",
    )(o_a, cb, o_c, proj, proj, proj, x, g_m, wa, wb, bb, wc, wo)


def _first_index_of_max(vals, idx, size):
    mx = jnp.max(vals, axis=0, keepdims=True)
    first = jnp.min(jnp.where(vals == mx, idx, size), axis=0, keepdims=True)
    return mx, first


def _router_kernel(x_ref, g_ref, sc_ref, sh_ref, wh_ref, wl_ref, b_ref, h_ref, gate_ref):
    h = _ada_rms(x_ref[...], g_ref[...], sc_ref[0], sh_ref[0])
    h_ref[...] = h.astype(BF16)
    h_hi, h_lo = _split_hi_lo(h)
    logits = _dot_nt(wh_ref[...], h_hi) + _dot_nt(wh_ref[...], h_lo) + _dot_nt(wl_ref[...], h_hi)
    scores = jax.nn.sigmoid(logits)
    biased = scores + b_ref[...]
    tm = scores.shape[1]
    minus_inf = -jnp.inf

    in_group = lax.broadcasted_iota(jnp.int32, (GROUP_SIZE, tm), 0)
    group_scores = []
    for g in range(N_GROUPS):
        vals = biased[g * GROUP_SIZE:(g + 1) * GROUP_SIZE]
        m1, i1 = _first_index_of_max(vals, in_group, GROUP_SIZE)
        m2 = jnp.max(jnp.where(in_group == i1, minus_inf, vals), axis=0, keepdims=True)
        group_scores.append(m1 + m2)
    gs = jnp.concatenate(group_scores, axis=0)

    gidx = lax.broadcasted_iota(jnp.int32, (N_GROUPS, tm), 0)
    group_sel = jnp.zeros((N_GROUPS, tm), jnp.bool_)
    for _ in range(TOPK_GROUPS):
        _, first = _first_index_of_max(gs, gidx, N_GROUPS)
        pick = gidx == first
        group_sel = jnp.logical_or(group_sel, pick)
        gs = jnp.where(pick, minus_inf, gs)

    eidx = lax.broadcasted_iota(jnp.int32, (N_EXPERTS, tm), 0)
    expert_group_sel = jnp.concatenate(
        [jnp.broadcast_to(group_sel[g:g + 1], (GROUP_SIZE, tm)) for g in range(N_GROUPS)], axis=0)
    cand = jnp.where(expert_group_sel, biased, NEG_INF)
    chosen = jnp.zeros((N_EXPERTS, tm), jnp.bool_)
    for _ in range(TOP_K):
        _, first = _first_index_of_max(cand, eidx, N_EXPERTS)
        pick = eidx == first
        chosen = jnp.logical_or(chosen, pick)
        cand = jnp.where(pick, minus_inf, cand)

    w = jnp.where(chosen, scores, 0.0)
    w = w / (jnp.sum(w, axis=0, keepdims=True) + 1e-20) * ROUTED_SCALE
    gate_ref[...] = w


def _router(x1, g, sc, sh, wr_hi, wr_lo, b_router, S):
    T, D = x1.shape
    tm = 512
    per_batch = S // tm
    E = N_EXPERTS
    return pl.pallas_call(
        _router_kernel,
        grid=(T // tm,),
        in_specs=[pl.BlockSpec((tm, D), lambda i: (i, 0)),
                  pl.BlockSpec((1, D), lambda i: (0, 0)),
                  pl.BlockSpec((1, 1, D), lambda i: (i // per_batch, 0, 0)),
                  pl.BlockSpec((1, 1, D), lambda i: (i // per_batch, 0, 0)),
                  pl.BlockSpec((E, D), lambda i: (0, 0)),
                  pl.BlockSpec((E, D), lambda i: (0, 0)),
                  pl.BlockSpec((E, 1), lambda i: (0, 0))],
        out_specs=[pl.BlockSpec((tm, D), lambda i: (i, 0)),
                   pl.BlockSpec((E, tm), lambda i: (0, i))],
        out_shape=[jax.ShapeDtypeStruct((T, D), BF16),
                   jax.ShapeDtypeStruct((E, T), F32)],
        compiler_params=_cparams(("arbitrary",)),
        name="router",
    )(x1, g, sc, sh, wr_hi, wr_lo, b_router)


def _moe_kernel(h_ref, gate_ref, x_ref, gf_ref, w1_ref, w3_ref, w2_ref, o_ref, acc, *, chunk):
    c = pl.program_id(1)

    @pl.when(c == 0)
    def _():
        acc[...] = jnp.zeros_like(acc)

    h = h_ref[...]
    gates = gate_ref[...]
    lane = lax.broadcasted_iota(jnp.int32, gates.shape, 1)
    total = jnp.zeros(acc.shape, F32)
    for k in range(chunk):
        e = c * chunk + k
        gcol = jnp.sum(jnp.where(lane == e, gates, 0.0), axis=1, keepdims=True)
        a = _dot(h, w1_ref[k])
        hid = a * jax.nn.sigmoid(a) * _dot(h, w3_ref[k]) * gcol
        total = total + _dot(hid.astype(BF16), w2_ref[k])
    acc[...] += total

    @pl.when(c == pl.num_programs(1) - 1)
    def _():
        o_ref[...] = x_ref[...] + gf_ref[0] * acc[...]


def _moe(h2, gates_tok, x1, g_f, w1, w3, w2, S):
    T, D = x1.shape
    n_exp = w1.shape[0]
    chunk = 5
    tm = 1024
    per_batch = S // tm
    F = EXPERT_FF
    return pl.pallas_call(
        functools.partial(_moe_kernel, chunk=chunk),
        grid=(T // tm, n_exp // chunk),
        in_specs=[pl.BlockSpec((tm, D), lambda i, c: (i, 0)),
                  pl.BlockSpec((tm, LANES), lambda i, c: (i, 0)),
                  pl.BlockSpec((tm, D), lambda i, c: (i, 0)),
                  pl.BlockSpec((1, 1, D), lambda i, c: (i // per_batch, 0, 0)),
                  pl.BlockSpec((chunk, D, F), lambda i, c: (c, 0, 0)),
                  pl.BlockSpec((chunk, D, F), lambda i, c: (c, 0, 0)),
                  pl.BlockSpec((chunk, F, D), lambda i, c: (c, 0, 0))],
        out_specs=pl.BlockSpec((tm, D), lambda i, c: (i, 0)),
        out_shape=jax.ShapeDtypeStruct((T, D), F32),
        scratch_shapes=[pltpu.VMEM((tm, D), F32)],
        compiler_params=_cparams(("arbitrary", "arbitrary")),
        name="moe_experts",
    )(h2, gates_tok, x1, g_f, w1, w3, w2)


def kernel(x, c, positions, w_mod, b_mod, norm_mix_g, norm_ffn_g, w_in, qn_g, kn_g, lam_q1, lam_k1, lam_q2, lam_k2, subln_g, w_proj_a, w_dw, b_dw, conv_ln_g, conv_ln_b, w_proj_b, b_proj_b, w_proj_c, w_out, w_router, b_router, w1, w3, w2, ws1, ws3, ws2):
    B, S, D = x.shape
    T = B * S
    depth = w_mod.shape[0]
    xt = x.reshape(T, D)

    inv = ROPE_THETA ** (-jnp.arange(0, DA_HEAD_DIM, 2, dtype=F32) / DA_HEAD_DIM)
    inv_lanes = jnp.tile(inv, LANES // (DA_HEAD_DIM // 2)).reshape(1, LANES)
    pos_lanes = jnp.broadcast_to(positions.astype(F32).reshape(T, 1), (T, LANES))
    cos, sin = _rope_tables(pos_lanes, inv_lanes)

    mod = _modulation(c, w_mod, b_mod)

    seg = jnp.arange(DA_WIDTH, dtype=jnp.int32) // DA_HEAD_DIM
    bd = jnp.where(seg[:, None] == seg[None, :], 1.0 / DA_HEAD_DIM, 0.0).astype(BF16)
    kk = jnp.arange(256, dtype=jnp.int32)
    u = (kk[:, None] > kk[None, :]).astype(BF16)
    n_seg = DA_WIDTH // DA_HEAD_DIM

    for l in range(depth):
        lambda_init = 0.8 - 0.6 * math.exp(-0.3 * l)
        sh_m, sc_m, g_m, sh_f, sc_f, g_f = [m.reshape(B, 1, D) for m in jnp.split(mod[l], 6, axis=-1)]

        proj = _norm_proj(xt, norm_mix_g[l].reshape(1, D), sc_m, sh_m, w_in[l].astype(BF16), S)

        q_r, k_r = _qk_prep(proj, cos, sin, bd,
                            jnp.tile(qn_g[l], n_seg).reshape(1, DA_WIDTH),
                            jnp.tile(kn_g[l], n_seg).reshape(1, DA_WIDTH))
        lam = (jnp.exp(jnp.sum(lam_q1[l] * lam_k1[l])) - jnp.exp(jnp.sum(lam_q2[l] * lam_k2[l])) + lambda_init)
        o_a = _diff_attn(jnp.full((1, LANES), lam, F32), q_r, k_r, proj,
                         subln_g[l].reshape(1, LANES), B, S, 1.0 - lambda_init)
        cb = _conv_module(proj, w_dw[l], b_dw[l].reshape(1, CONV_CH), conv_ln_g[l].reshape(1, CONV_CH),
                          conv_ln_b[l].reshape(1, CONV_CH), S)
        o_c = _sb_attn(proj, u, B, S)
        x1 = _merge(o_a, cb, o_c, proj, xt, g_m, w_proj_a[l].astype(BF16), w_proj_b[l].astype(BF16),
                    b_proj_b[l].reshape(1, D), w_proj_c[l].astype(BF16), w_out[l].astype(BF16), S)

        wr_t = w_router[l].T
        wr_hi = wr_t.astype(BF16)
        wr_lo = (wr_t - wr_hi.astype(F32)).astype(BF16)
        h2, gates = _router(x1, norm_ffn_g[l].reshape(1, D), sc_f, sh_f, wr_hi, wr_lo,
                            b_router[l].reshape(N_EXPERTS, 1), S)
        gates_tok = jnp.concatenate(
            [gates.T, jnp.ones((T, 1), F32), jnp.zeros((T, LANES - N_EXPERTS - 1), F32)], axis=1)
        w1_all = jnp.concatenate([w1[l], ws1[l][None]], axis=0).astype(BF16)
        w3_all = jnp.concatenate([w3[l], ws3[l][None]], axis=0).astype(BF16)
        w2_all = jnp.concatenate([w2[l], ws2[l][None]], axis=0).astype(BF16)
        xt = _moe(h2, gates_tok, x1, g_f, w1_all, w3_all, w2_all, S)

    return xt.reshape(B, S, D)
```

```python
import functools
import math

import jax
import jax.numpy as jnp
from jax import lax
from jax.experimental import pallas as pl
from jax.experimental.pallas import tpu as pltpu

F32 = jnp.float32
BF16 = jnp.bfloat16

D_MODEL = 1024
DA_HEADS = 4
DA_HEAD_DIM = 64
DA_WIDTH = 512
CONV_CH = 512
CONV_WIDTH = 31
SB_WIDTH = 512
SB_HEAD_DIM = 64
IN_COLS = 7168
ROPE_THETA = 10000.0
EPS = 1e-6
NEG_INF = -1e30
LOG2E = math.log2(math.e)
SB_BLOCK = 256
SB_BLOCKS_PER_TRIP = 4
N_EXPERTS = 64
TOP_K = 8
N_GROUPS = 8
TOPK_GROUPS = 4
GROUP_SIZE = N_EXPERTS // N_GROUPS
EXPERT_FF = 256
ROUTED_SCALE = 2.5

LANES = 128
CONV_HALO = 32

COL_QA, COL_KA, COL_VA, COL_UB, COL_QC, COL_KC, COL_VC, COL_GATES = 0, 512, 1024, 1536, 2560, 3072, 3584, 4096

VMEM_LIMIT = 48 * 1024 * 1024


def _cparams(sem):
    return pltpu.CompilerParams(dimension_semantics=sem, vmem_limit_bytes=VMEM_LIMIT)


def _dot(a, b):
    return jnp.dot(a, b, preferred_element_type=F32)


def _dot_nt(a, b):
    return lax.dot_general(a, b, (((1,), (1,)), ((), ())), preferred_element_type=F32)


def _split_hi_lo(x):
    hi = x.astype(BF16)
    lo = (x - hi.astype(F32)).astype(BF16)
    return hi, lo


def _mod_kernel(c_ref, w_ref, b_ref, o_ref):
    c = c_ref[...]
    c_act = c * jax.nn.sigmoid(c)
    o_ref[0] = jnp.dot(c_act, w_ref[0], precision=lax.Precision.HIGHEST,
                       preferred_element_type=F32) + b_ref[0]


def _modulation(c, w_mod, b_mod):
    L, D, N = w_mod.shape
    B = c.shape[0]
    tn = 1536
    return pl.pallas_call(
        _mod_kernel,
        grid=(L, N // tn),
        in_specs=[pl.BlockSpec((B, D), lambda l, j: (0, 0)),
                  pl.BlockSpec((1, D, tn), lambda l, j: (l, 0, j)),
                  pl.BlockSpec((1, 1, tn), lambda l, j: (l, 0, j))],
        out_specs=pl.BlockSpec((1, B, tn), lambda l, j: (l, 0, j)),
        out_shape=jax.ShapeDtypeStruct((L, B, N), F32),
        compiler_params=_cparams(("arbitrary", "arbitrary")),
        name="modulation",
    )(c, w_mod, b_mod.reshape(L, 1, N))


def _rope_kernel(pos_ref, inv_ref, cos_ref, sin_ref):
    ang = pos_ref[...] * inv_ref[...]
    lane = lax.broadcasted_iota(jnp.int32, ang.shape, 1)
    first_half = (lane % DA_HEAD_DIM) < (DA_HEAD_DIM // 2)
    cos_ref[...] = jnp.cos(ang)
    s = jnp.sin(ang)
    sin_ref[...] = jnp.where(first_half, -s, s)


def _rope_tables(pos_lanes, inv_lanes):
    T = pos_lanes.shape[0]
    tm = 2048
    return pl.pallas_call(
        _rope_kernel,
        grid=(T // tm,),
        in_specs=[pl.BlockSpec((tm, LANES), lambda i: (i, 0)),
                  pl.BlockSpec((1, LANES), lambda i: (0, 0))],
        out_specs=[pl.BlockSpec((tm, LANES), lambda i: (i, 0))] * 2,
        out_shape=[jax.ShapeDtypeStruct((T, LANES), F32)] * 2,
        compiler_params=_cparams(("arbitrary",)),
        name="rope_tables",
    )(pos_lanes, inv_lanes)


def _ada_rms(x, g, sc, sh):
    ms = jnp.mean(x * x, axis=-1, keepdims=True)
    return x * lax.rsqrt(ms + EPS) * g * (1.0 + sc) + sh


def _norm_proj_kernel(x_ref, g_ref, sc_ref, sh_ref, w_ref, o_ref, h_scr):
    @pl.when(pl.program_id(1) == 0)
    def _():
        h_scr[...] = _ada_rms(x_ref[...], g_ref[...], sc_ref[0], sh_ref[0]).astype(BF16)

    o_ref[...] = _dot(h_scr[...], w_ref[...]).astype(BF16)


def _norm_proj(x, g, sc, sh, w_bf16, S):
    T, D = x.shape
    N = w_bf16.shape[1]
    tm, tn = 1024, 1024
    per_batch = S // tm
    return pl.pallas_call(
        _norm_proj_kernel,
        grid=(T // tm, N // tn),
        in_specs=[pl.BlockSpec((tm, D), lambda i, j: (i, 0)),
                  pl.BlockSpec((1, D), lambda i, j: (0, 0)),
                  pl.BlockSpec((1, 1, D), lambda i, j: (i // per_batch, 0, 0)),
                  pl.BlockSpec((1, 1, D), lambda i, j: (i // per_batch, 0, 0)),
                  pl.BlockSpec((D, tn), lambda i, j: (0, j))],
        out_specs=pl.BlockSpec((tm, tn), lambda i, j: (i, j)),
        out_shape=jax.ShapeDtypeStruct((T, N), BF16),
        scratch_shapes=[pltpu.VMEM((tm, D), BF16)],
        compiler_params=_cparams(("arbitrary", "arbitrary")),
        name="norm_proj",
    )(x, g, sc, sh, w_bf16)


def _qk_prep_kernel(q_ref, k_ref, cos_ref, sin_ref, bd_ref, qg_ref, kg_ref, qo_ref, ko_ref):
    cos = cos_ref[...]
    sin = sin_ref[...]
    bd = bd_ref[...]
    lane = lax.broadcasted_iota(jnp.int32, cos.shape, 1)
    first_half = (lane % DA_HEAD_DIM) < (DA_HEAD_DIM // 2)
    half = DA_HEAD_DIM // 2

    def prep(x_ref, g_ref, o_ref, scale):
        x = x_ref[...].astype(F32)
        hi, lo = _split_hi_lo(x * x)
        ms = _dot(hi, bd) + _dot(lo, bd)
        y = x * lax.rsqrt(ms + EPS) * g_ref[...]
        for c in range(DA_WIDTH // LANES):
            yc = y[:, c * LANES:(c + 1) * LANES]
            swapped = jnp.where(first_half, pltpu.roll(yc, LANES - half, 1), pltpu.roll(yc, half, 1))
            o_ref[:, c * LANES:(c + 1) * LANES] = ((yc * cos + swapped * sin) * scale).astype(BF16)

    prep(q_ref, qg_ref, qo_ref, DA_HEAD_DIM ** -0.5)
    prep(k_ref, kg_ref, ko_ref, 1.0)


def _qk_prep(proj, cos, sin, bd, qg, kg):
    T = proj.shape[0]
    tm = 1024
    W = DA_WIDTH
    return pl.pallas_call(
        _qk_prep_kernel,
        grid=(T // tm,),
        in_specs=[pl.BlockSpec((tm, W), lambda i: (i, COL_QA // W)),
                  pl.BlockSpec((tm, W), lambda i: (i, COL_KA // W)),
                  pl.BlockSpec((tm, LANES), lambda i: (i, 0)),
                  pl.BlockSpec((tm, LANES), lambda i: (i, 0)),
                  pl.BlockSpec((W, W), lambda i: (0, 0)),
                  pl.BlockSpec((1, W), lambda i: (0, 0)),
                  pl.BlockSpec((1, W), lambda i: (0, 0))],
        out_specs=[pl.BlockSpec((tm, W), lambda i: (i, 0))] * 2,
        out_shape=[jax.ShapeDtypeStruct((T, W), BF16)] * 2,
        compiler_params=_cparams(("arbitrary",)),
        name="qk_prep",
    )(proj, proj, cos, sin, bd, qg, kg)


def _diff_attn_kernel(lam_ref, q_ref, k_ref, v_ref, g_ref, o_ref, acc1, acc2, *, tq, out_scale):
    qi = pl.program_id(2)
    qf = q_ref[...].astype(F32)
    lane = lax.broadcasted_iota(jnp.int32, qf.shape, 1)
    q1 = jnp.where(lane < DA_HEAD_DIM, qf, 0.0).astype(BF16)
    q2 = jnp.where(lane >= DA_HEAD_DIM, qf, 0.0).astype(BF16)
    acc1[...] = jnp.zeros_like(acc1)
    acc2[...] = jnp.zeros_like(acc2)
    row = lax.broadcasted_iota(jnp.int32, (tq, tq), 0)
    col = lax.broadcasted_iota(jnp.int32, (tq, tq), 1)

    def one_map(qm, k, v, m, l, acc, masked):
        s = _dot_nt(qm, k)
        if masked:
            s = jnp.where(col <= row, s, NEG_INF)
        m_new = jnp.maximum(m, jnp.max(s, axis=1, keepdims=True))
        p = jnp.exp(s - m_new)
        alpha = jnp.exp(m - m_new)
        acc[...] = alpha * acc[...] + _dot(p.astype(BF16), v)
        return m_new, alpha * l + jnp.sum(p, axis=1, keepdims=True)

    def step(kj, carry, masked):
        m1, l1, m2, l2 = carry
        start = pl.multiple_of(kj * tq, tq)
        k = k_ref[pl.ds(start, tq), :]
        v = v_ref[pl.ds(start, tq), :]
        m1, l1 = one_map(q1, k, v, m1, l1, acc1, masked)
        m2, l2 = one_map(q2, k, v, m2, l2, acc2, masked)
        return m1, l1, m2, l2

    neg = jnp.full((tq, 1), NEG_INF, F32)
    zero = jnp.zeros((tq, 1), F32)
    carry = lax.fori_loop(0, qi, lambda kj, c: step(kj, c, False), (neg, zero, neg, zero))
    _, l1, _, l2 = step(qi, carry, True)
    o = acc1[...] / l1 - lam_ref[...] * (acc2[...] / l2)
    ms = jnp.mean(o * o, axis=-1, keepdims=True)
    o_ref[...] = (o * lax.rsqrt(ms + EPS) * g_ref[...] * out_scale).astype(BF16)


def _diff_attn(lam_lanes, q_r, k_r, proj, subln_g, B, S, out_scale):
    T = q_r.shape[0]
    tq = 512
    nq = S // tq
    return pl.pallas_call(
        functools.partial(_diff_attn_kernel, tq=tq, out_scale=out_scale),
        grid=(B, DA_HEADS, nq),
        in_specs=[pl.BlockSpec((1, LANES), lambda b, h, i: (0, 0)),
                  pl.BlockSpec((tq, LANES), lambda b, h, i: (b * nq + i, h)),
                  pl.BlockSpec((S, LANES), lambda b, h, i: (b, h)),
                  pl.BlockSpec((S, LANES), lambda b, h, i: (b, COL_VA // LANES + h)),
                  pl.BlockSpec((1, LANES), lambda b, h, i: (0, 0))],
        out_specs=pl.BlockSpec((tq, LANES), lambda b, h, i: (b * nq + i, h)),
        out_shape=jax.ShapeDtypeStruct((T, DA_WIDTH), BF16),
        scratch_shapes=[pltpu.VMEM((tq, LANES), F32), pltpu.VMEM((tq, LANES), F32)],
        compiler_params=_cparams(("arbitrary", "arbitrary", "arbitrary")),
        name="diff_attn",
    )(lam_lanes, q_r, k_r, proj, subln_g)


def _sb_attn_kernel(q_ref, k_ref, v_ref, u_ref, o_ref, acc, *, tq):
    qi = pl.program_id(2)
    qf = q_ref[...].astype(F32) * (SB_HEAD_DIM ** -0.5 * LOG2E)
    lane = lax.broadcasted_iota(jnp.int32, qf.shape, 1)
    qs = (jnp.where(lane < SB_HEAD_DIM, qf, 0.0).astype(BF16),
          jnp.where(lane >= SB_HEAD_DIM, qf, 0.0).astype(BF16))
    uu = u_ref[...]
    acc[...] = jnp.zeros_like(acc)
    row = lax.broadcasted_iota(jnp.int32, (tq, tq), 0)
    col = lax.broadcasted_iota(jnp.int32, (tq, tq), 1)
    before = col < row

    def step(kjs, rs, masked):
        ks = [k_ref[pl.ds(pl.multiple_of(kj * tq, tq), tq), :] for kj in kjs]
        vs = [v_ref[pl.ds(pl.multiple_of(kj * tq, tq), tq), :] for kj in kjs]
        chains = [(h, b) for b in range(len(kjs)) for h in range(2)]
        zs = [_dot_nt(qs[h], ks[b]) for h, b in chains]
        ps, hls = [], []
        for z in zs:
            neg_abs = pltpu.bitcast(pltpu.bitcast(z, jnp.uint32) | jnp.uint32(0x80000000), F32)
            p = jnp.maximum(z, 0.0) + jnp.log2(1.0 + jnp.exp2(neg_abs))
            if masked:
                p = jnp.where(before, p, 0.0)
            hi, lo = _split_hi_lo(p)
            ps.append(jnp.sum(p, axis=1, keepdims=True))
            hls.append(jnp.concatenate([hi, lo], axis=1))
        mms = [_dot(hl, uu) for hl in hls]
        rs = list(rs)
        for c, (h, b) in enumerate(chains):
            a = jnp.exp2(zs[c] + mms[c] + rs[h])
            if masked:
                a = jnp.where(before, a, 0.0)
            acc[h * tq:(h + 1) * tq, :] += _dot(a.astype(BF16), vs[b])
            rs[h] = rs[h] - ps[c]
        return tuple(rs)

    zero = jnp.zeros((tq, 1), F32)
    rs = step([qi], (zero, zero), True)
    def run(first, count, c):
        return step([first - d for d in range(count)], c, False)

    nb = SB_BLOCKS_PER_TRIP
    rs = lax.fori_loop(0, qi // nb, lambda t, c: run(qi - 1 - nb * t, nb, c), rs)
    left = qi % nb
    part = nb // 2
    while part >= 1:
        rs = lax.cond(left & part != 0, functools.partial(run, (left & (2 * part - 1)) - 1, part),
                      lambda c: c, rs)
        part //= 2

    o_ref[...] = jnp.where(lane < SB_HEAD_DIM, acc[0:tq, :], acc[tq:2 * tq, :]).astype(BF16)


def _sb_attn(proj, u, B, S):
    T = proj.shape[0]
    tq = SB_BLOCK
    nq = S // tq
    return pl.pallas_call(
        functools.partial(_sb_attn_kernel, tq=tq),
        grid=(B, SB_WIDTH // LANES, nq),
        in_specs=[pl.BlockSpec((tq, LANES), lambda b, p, i: (b * nq + i, COL_QC // LANES + p)),
                  pl.BlockSpec((S, LANES), lambda b, p, i: (b, COL_KC // LANES + p)),
                  pl.BlockSpec((S, LANES), lambda b, p, i: (b, COL_VC // LANES + p)),
                  pl.BlockSpec((2 * tq, tq), lambda b, p, i: (0, 0))],
        out_specs=pl.BlockSpec((tq, LANES), lambda b, p, i: (b * nq + i, p)),
        out_shape=jax.ShapeDtypeStruct((T, SB_WIDTH), BF16),
        scratch_shapes=[pltpu.VMEM((2 * tq, LANES), F32)],
        compiler_params=_cparams(("arbitrary", "arbitrary", "arbitrary")),
        name="sb_attn",
    )(proj, proj, proj, u)


def _conv_kernel(a_ref, g_ref, ah_ref, gh_ref, w_ref, b_ref, lg_ref, lb_ref, o_ref, hbuf, *, tm, per_batch, rows):
    i = pl.program_id(0)
    halo = ah_ref[...].astype(F32) * jax.nn.sigmoid(gh_ref[...].astype(F32))
    hbuf[0:CONV_HALO, :] = jnp.where(i % per_batch == 0, 0.0, halo)
    hbuf[CONV_HALO:, :] = a_ref[...].astype(F32) * jax.nn.sigmoid(g_ref[...].astype(F32))
    w = w_ref[...]
    first = CONV_HALO - (CONV_WIDTH - 1)
    for r0 in range(0, tm, rows):
        acc = jnp.broadcast_to(b_ref[...], (rows, CONV_CH))
        for t in range(CONV_WIDTH):
            acc = acc + hbuf[r0 + first + t:r0 + first + t + rows, :] * w[t:t + 1, :]
        mu = jnp.mean(acc, axis=-1, keepdims=True)
        d = acc - mu
        var = jnp.mean(d * d, axis=-1, keepdims=True)
        y = d * lax.rsqrt(var + EPS) * lg_ref[...] + lb_ref[...]
        o_ref[r0:r0 + rows, :] = (y * jax.nn.sigmoid(y)).astype(BF16)


def _conv_module(proj, w_dw, b_dw, ln_g, ln_b, S):
    T = proj.shape[0]
    tm = 512
    per_batch = S // tm
    C = CONV_CH
    ca, cg = COL_UB // C, COL_UB // C + 1
    halo_blocks = tm // CONV_HALO

    def halo_idx(col):
        return lambda i: (jnp.maximum(i * halo_blocks - 1, 0), col)

    return pl.pallas_call(
        functools.partial(_conv_kernel, tm=tm, per_batch=per_batch, rows=64),
        grid=(T // tm,),
        in_specs=[pl.BlockSpec((tm, C), lambda i: (i, ca)),
                  pl.BlockSpec((tm, C), lambda i: (i, cg)),
                  pl.BlockSpec((CONV_HALO, C), halo_idx(ca)),
                  pl.BlockSpec((CONV_HALO, C), halo_idx(cg)),
                  pl.BlockSpec((CONV_WIDTH, C), lambda i: (0, 0)),
                  pl.BlockSpec((1, C), lambda i: (0, 0)),
                  pl.BlockSpec((1, C), lambda i: (0, 0)),
                  pl.BlockSpec((1, C), lambda i: (0, 0))],
        out_specs=pl.BlockSpec((tm, C), lambda i: (i, 0)),
        out_shape=jax.ShapeDtypeStruct((T, C), BF16),
        scratch_shapes=[pltpu.VMEM((tm + CONV_HALO, C), F32)],
        compiler_params=_cparams(("arbitrary",)),
        name="conv_module",
    )(proj, proj, proj, proj, w_dw, b_dw, ln_g, ln_b)


def _merge_kernel(oa_ref, cb_ref, oc_ref, g0_ref, g1_ref, g2_ref, x_ref, gm_ref,
                  wa_ref, wb_ref, bb_ref, wc_ref, wo_ref, o_ref):
    y_a = _dot(oa_ref[...], wa_ref[...])
    y_b = _dot(cb_ref[...], wb_ref[...]) + bb_ref[...]
    y_c = _dot(oc_ref[...], wc_ref[...])
    merged = (jax.nn.sigmoid(g0_ref[...].astype(F32)) * y_a
              + jax.nn.sigmoid(g1_ref[...].astype(F32)) * y_b
              + jax.nn.sigmoid(g2_ref[...].astype(F32)) * y_c)
    o_ref[...] = x_ref[...] + gm_ref[0] * _dot(merged.astype(BF16), wo_ref[...])


def _merge(o_a, cb, o_c, proj, x, g_m, wa, wb, bb, wc, wo, S):
    T, D = x.shape
    tm = 512
    per_batch = S // tm
    W = DA_WIDTH
    gcol = COL_GATES // D
    branch = pl.BlockSpec((tm, W), lambda i: (i, 0))
    wspec = pl.BlockSpec((W, D), lambda i: (0, 0))
    return pl.pallas_call(
        _merge_kernel,
        grid=(T // tm,),
        in_specs=[branch, branch, branch,
                  pl.BlockSpec((tm, D), lambda i: (i, gcol)),
                  pl.BlockSpec((tm, D), lambda i: (i, gcol + 1)),
                  pl.BlockSpec((tm, D), lambda i: (i, gcol + 2)),
                  pl.BlockSpec((tm, D), lambda i: (i, 0)),
                  pl.BlockSpec((1, 1, D), lambda i: (i // per_batch, 0, 0)),
                  wspec, wspec, pl.BlockSpec((1, D), lambda i: (0, 0)), wspec,
                  pl.BlockSpec((D, D), lambda i: (0, 0))],
        out_specs=pl.BlockSpec((tm, D), lambda i: (i, 0)),
        out_shape=jax.ShapeDtypeStruct((T, D), F32),
        compiler_params=_cparams(("arbitrary",)),
        name="merge",
    )(o_a, cb, o_c, proj, proj, proj, x, g_m, wa, wb, bb, wc, wo)


def _first_index_of_max(vals, idx, size):
    mx = jnp.max(vals, axis=0, keepdims=True)
    first = jnp.min(jnp.where(vals == mx, idx, size), axis=0, keepdims=True)
    return mx, first


def _router_kernel(x_ref, g_ref, sc_ref, sh_ref, wh_ref, wl_ref, b_ref, h_ref, gate_ref):
    h = _ada_rms(x_ref[...], g_ref[...], sc_ref[0], sh_ref[0])
    h_ref[...] = h.astype(BF16)
    h_hi, h_lo = _split_hi_lo(h)
    logits = _dot_nt(wh_ref[...], h_hi) + _dot_nt(wh_ref[...], h_lo) + _dot_nt(wl_ref[...], h_hi)
    scores = jax.nn.sigmoid(logits)
    biased = scores + b_ref[...]
    tm = scores.shape[1]
    minus_inf = -jnp.inf

    in_group = lax.broadcasted_iota(jnp.int32, (GROUP_SIZE, tm), 0)
    group_scores = []
    for g in range(N_GROUPS):
        vals = biased[g * GROUP_SIZE:(g + 1) * GROUP_SIZE]
        m1, i1 = _first_index_of_max(vals, in_group, GROUP_SIZE)
        m2 = jnp.max(jnp.where(in_group == i1, minus_inf, vals), axis=0, keepdims=True)
        group_scores.append(m1 + m2)
    gs = jnp.concatenate(group_scores, axis=0)

    gidx = lax.broadcasted_iota(jnp.int32, (N_GROUPS, tm), 0)
    group_sel = jnp.zeros((N_GROUPS, tm), jnp.bool_)
    for _ in range(TOPK_GROUPS):
        _, first = _first_index_of_max(gs, gidx, N_GROUPS)
        pick = gidx == first
        group_sel = jnp.logical_or(group_sel, pick)
        gs = jnp.where(pick, minus_inf, gs)

    eidx = lax.broadcasted_iota(jnp.int32, (N_EXPERTS, tm), 0)
    expert_group_sel = jnp.concatenate(
        [jnp.broadcast_to(group_sel[g:g + 1], (GROUP_SIZE, tm)) for g in range(N_GROUPS)], axis=0)
    cand = jnp.where(expert_group_sel, biased, NEG_INF)
    chosen = jnp.zeros((N_EXPERTS, tm), jnp.bool_)
    for _ in range(TOP_K):
        _, first = _first_index_of_max(cand, eidx, N_EXPERTS)
        pick = eidx == first
        chosen = jnp.logical_or(chosen, pick)
        cand = jnp.where(pick, minus_inf, cand)

    w = jnp.where(chosen, scores, 0.0)
    w = w / (jnp.sum(w, axis=0, keepdims=True) + 1e-20) * ROUTED_SCALE
    gate_ref[...] = w


def _router(x1, g, sc, sh, wr_hi, wr_lo, b_router, S):
    T, D = x1.shape
    tm = 512
    per_batch = S // tm
    E = N_EXPERTS
    return pl.pallas_call(
        _router_kernel,
        grid=(T // tm,),
        in_specs=[pl.BlockSpec((tm, D), lambda i: (i, 0)),
                  pl.BlockSpec((1, D), lambda i: (0, 0)),
                  pl.BlockSpec((1, 1, D), lambda i: (i // per_batch, 0, 0)),
                  pl.BlockSpec((1, 1, D), lambda i: (i // per_batch, 0, 0)),
                  pl.BlockSpec((E, D), lambda i: (0, 0)),
                  pl.BlockSpec((E, D), lambda i: (0, 0)),
                  pl.BlockSpec((E, 1), lambda i: (0, 0))],
        out_specs=[pl.BlockSpec((tm, D), lambda i: (i, 0)),
                   pl.BlockSpec((E, tm), lambda i: (0, i))],
        out_shape=[jax.ShapeDtypeStruct((T, D), BF16),
                   jax.ShapeDtypeStruct((E, T), F32)],
        compiler_params=_cparams(("arbitrary",)),
        name="router",
    )(x1, g, sc, sh, wr_hi, wr_lo, b_router)


def _moe_kernel(h_ref, gate_ref, x_ref, gf_ref, w1_ref, w3_ref, w2_ref, o_ref, acc, *, chunk):
    c = pl.program_id(1)

    @pl.when(c == 0)
    def _():
        acc[...] = jnp.zeros_like(acc)

    h = h_ref[...]
    gates = gate_ref[...]
    lane = lax.broadcasted_iota(jnp.int32, gates.shape, 1)
    total = jnp.zeros(acc.shape, F32)
    for k in range(chunk):
        e = c * chunk + k
        gcol = jnp.sum(jnp.where(lane == e, gates, 0.0), axis=1, keepdims=True)
        a = _dot(h, w1_ref[k])
        hid = a * jax.nn.sigmoid(a) * _dot(h, w3_ref[k]) * gcol
        total = total + _dot(hid.astype(BF16), w2_ref[k])
    acc[...] += total

    @pl.when(c == pl.num_programs(1) - 1)
    def _():
        o_ref[...] = x_ref[...] + gf_ref[0] * acc[...]


def _moe(h2, gates_tok, x1, g_f, w1, w3, w2, S):
    T, D = x1.shape
    n_exp = w1.shape[0]
    chunk = 5
    tm = 1024
    per_batch = S // tm
    F = EXPERT_FF
    return pl.pallas_call(
        functools.partial(_moe_kernel, chunk=chunk),
        grid=(T // tm, n_exp // chunk),
        in_specs=[pl.BlockSpec((tm, D), lambda i, c: (i, 0)),
                  pl.BlockSpec((tm, LANES), lambda i, c: (i, 0)),
                  pl.BlockSpec((tm, D), lambda i, c: (i, 0)),
                  pl.BlockSpec((1, 1, D), lambda i, c: (i // per_batch, 0, 0)),
                  pl.BlockSpec((chunk, D, F), lambda i, c: (c, 0, 0)),
                  pl.BlockSpec((chunk, D, F), lambda i, c: (c, 0, 0)),
                  pl.BlockSpec((chunk, F, D), lambda i, c: (c, 0, 0))],
        out_specs=pl.BlockSpec((tm, D), lambda i, c: (i, 0)),
        out_shape=jax.ShapeDtypeStruct((T, D), F32),
        scratch_shapes=[pltpu.VMEM((tm, D), F32)],
        compiler_params=_cparams(("arbitrary", "arbitrary")),
        name="moe_experts",
    )(h2, gates_tok, x1, g_f, w1, w3, w2)


def kernel(x, c, positions, w_mod, b_mod, norm_mix_g, norm_ffn_g, w_in, qn_g, kn_g, lam_q1, lam_k1, lam_q2, lam_k2, subln_g, w_proj_a, w_dw, b_dw, conv_ln_g, conv_ln_b, w_proj_b, b_proj_b, w_proj_c, w_out, w_router, b_router, w1, w3, w2, ws1, ws3, ws2):
    B, S, D = x.shape
    T = B * S
    depth = w_mod.shape[0]
    xt = x.reshape(T, D)

    inv = ROPE_THETA ** (-jnp.arange(0, DA_HEAD_DIM, 2, dtype=F32) / DA_HEAD_DIM)
    inv_lanes = jnp.tile(inv, LANES // (DA_HEAD_DIM // 2)).reshape(1, LANES)
    pos_lanes = jnp.broadcast_to(positions.astype(F32).reshape(T, 1), (T, LANES))
    cos, sin = _rope_tables(pos_lanes, inv_lanes)

    mod = _modulation(c, w_mod, b_mod)

    seg = jnp.arange(DA_WIDTH, dtype=jnp.int32) // DA_HEAD_DIM
    bd = jnp.where(seg[:, None] == seg[None, :], 1.0 / DA_HEAD_DIM, 0.0).astype(BF16)
    kk = jnp.arange(SB_BLOCK, dtype=jnp.int32)
    u = jnp.where(kk[:, None] >= kk[None, :], -1.0, 0.0).astype(BF16)
    u = jnp.concatenate([u, u], axis=0)
    n_seg = DA_WIDTH // DA_HEAD_DIM

    for l in range(depth):
        lambda_init = 0.8 - 0.6 * math.exp(-0.3 * l)
        sh_m, sc_m, g_m, sh_f, sc_f, g_f = [m.reshape(B, 1, D) for m in jnp.split(mod[l], 6, axis=-1)]

        proj = _norm_proj(xt, norm_mix_g[l].reshape(1, D), sc_m, sh_m, w_in[l].astype(BF16), S)

        q_r, k_r = _qk_prep(proj, cos, sin, bd,
                            jnp.tile(qn_g[l], n_seg).reshape(1, DA_WIDTH),
                            jnp.tile(kn_g[l], n_seg).reshape(1, DA_WIDTH))
        lam = (jnp.exp(jnp.sum(lam_q1[l] * lam_k1[l])) - jnp.exp(jnp.sum(lam_q2[l] * lam_k2[l])) + lambda_init)
        o_a = _diff_attn(jnp.full((1, LANES), lam, F32), q_r, k_r, proj,
                         subln_g[l].reshape(1, LANES), B, S, 1.0 - lambda_init)
        cb = _conv_module(proj, w_dw[l], b_dw[l].reshape(1, CONV_CH), conv_ln_g[l].reshape(1, CONV_CH),
                          conv_ln_b[l].reshape(1, CONV_CH), S)
        o_c = _sb_attn(proj, u, B, S)
        x1 = _merge(o_a, cb, o_c, proj, xt, g_m, w_proj_a[l].astype(BF16), w_proj_b[l].astype(BF16),
                    b_proj_b[l].reshape(1, D), w_proj_c[l].astype(BF16), w_out[l].astype(BF16), S)

        wr_t = w_router[l].T
        wr_hi = wr_t.astype(BF16)
        wr_lo = (wr_t - wr_hi.astype(F32)).astype(BF16)
        h2, gates = _router(x1, norm_ffn_g[l].reshape(1, D), sc_f, sh_f, wr_hi, wr_lo,
                            b_router[l].reshape(N_EXPERTS, 1), S)
        gates_tok = jnp.concatenate(
            [gates.T, jnp.ones((T, 1), F32), jnp.zeros((T, LANES - N_EXPERTS - 1), F32)], axis=1)
        w1_all = jnp.concatenate([w1[l], ws1[l][None]], axis=0).astype(BF16)
        w3_all = jnp.concatenate([w3[l], ws3[l][None]], axis=0).astype(BF16)
        w2_all = jnp.concatenate([w2[l], ws2[l][None]], axis=0).astype(BF16)
        xt = _moe(h2, gates_tok, x1, g_f, w1_all, w3_all, w2_all, S)

    return xt.reshape(B, S, D)
```

```python
import functools
import math

import jax
import jax.numpy as jnp
from jax import lax
from jax.experimental import pallas as pl
from jax.experimental.pallas import tpu as pltpu

F32 = jnp.float32
BF16 = jnp.bfloat16

D_MODEL = 1024
DA_HEADS = 4
DA_HEAD_DIM = 64
DA_WIDTH = 512
CONV_CH = 512
CONV_WIDTH = 31
SB_WIDTH = 512
SB_HEAD_DIM = 64
IN_COLS = 7168
ROPE_THETA = 10000.0
EPS = 1e-6
NEG_INF = -1e30
LOG2E = math.log2(math.e)
MAX_FIXED_SHIFT = 40.0
SB_BLOCK = 256
SB_BLOCKS_PER_TRIP = 4
N_EXPERTS = 64
TOP_K = 8
N_GROUPS = 8
TOPK_GROUPS = 4
GROUP_SIZE = N_EXPERTS // N_GROUPS
EXPERT_FF = 256
ROUTED_SCALE = 2.5

LANES = 128
SUBLANES = 8
CONV_HALO = 32

COL_QA, COL_KA, COL_VA, COL_UB, COL_QC, COL_KC, COL_VC, COL_GATES = 0, 512, 1024, 1536, 2560, 3072, 3584, 4096

VMEM_LIMIT = 48 * 1024 * 1024


def _cparams(sem):
    return pltpu.CompilerParams(dimension_semantics=sem, vmem_limit_bytes=VMEM_LIMIT)


def _dot(a, b):
    return jnp.dot(a, b, preferred_element_type=F32)


def _dot_nt(a, b):
    return lax.dot_general(a, b, (((1,), (1,)), ((), ())), preferred_element_type=F32)


def _split_hi_lo(x):
    hi = x.astype(BF16)
    lo = (x - hi.astype(F32)).astype(BF16)
    return hi, lo


def _mod_kernel(c_ref, w_ref, b_ref, o_ref):
    c = c_ref[...]
    c_act = c * jax.nn.sigmoid(c)
    o_ref[0] = jnp.dot(c_act, w_ref[0], precision=lax.Precision.HIGHEST,
                       preferred_element_type=F32) + b_ref[0]


def _modulation(c, w_mod, b_mod):
    L, D, N = w_mod.shape
    B = c.shape[0]
    tn = 1536
    return pl.pallas_call(
        _mod_kernel,
        grid=(L, N // tn),
        in_specs=[pl.BlockSpec((B, D), lambda l, j: (0, 0)),
                  pl.BlockSpec((1, D, tn), lambda l, j: (l, 0, j)),
                  pl.BlockSpec((1, 1, tn), lambda l, j: (l, 0, j))],
        out_specs=pl.BlockSpec((1, B, tn), lambda l, j: (l, 0, j)),
        out_shape=jax.ShapeDtypeStruct((L, B, N), F32),
        compiler_params=_cparams(("arbitrary", "arbitrary")),
        name="modulation",
    )(c, w_mod, b_mod.reshape(L, 1, N))


def _rope_kernel(pos_ref, inv_ref, cos_ref, sin_ref):
    ang = pos_ref[...] * inv_ref[...]
    lane = lax.broadcasted_iota(jnp.int32, ang.shape, 1)
    first_half = (lane % DA_HEAD_DIM) < (DA_HEAD_DIM // 2)
    cos_ref[...] = jnp.cos(ang)
    s = jnp.sin(ang)
    sin_ref[...] = jnp.where(first_half, -s, s)


def _rope_tables(pos_lanes, inv_lanes):
    T = pos_lanes.shape[0]
    tm = 2048
    return pl.pallas_call(
        _rope_kernel,
        grid=(T // tm,),
        in_specs=[pl.BlockSpec((tm, LANES), lambda i: (i, 0)),
                  pl.BlockSpec((1, LANES), lambda i: (0, 0))],
        out_specs=[pl.BlockSpec((tm, LANES), lambda i: (i, 0))] * 2,
        out_shape=[jax.ShapeDtypeStruct((T, LANES), F32)] * 2,
        compiler_params=_cparams(("arbitrary",)),
        name="rope_tables",
    )(pos_lanes, inv_lanes)


def _ada_rms(x, g, sc, sh):
    ms = jnp.mean(x * x, axis=-1, keepdims=True)
    return x * lax.rsqrt(ms + EPS) * g * (1.0 + sc) + sh


def _norm_proj_kernel(x_ref, g_ref, sc_ref, sh_ref, w_ref, o_ref, h_scr):
    @pl.when(pl.program_id(1) == 0)
    def _():
        h_scr[...] = _ada_rms(x_ref[...], g_ref[...], sc_ref[0], sh_ref[0]).astype(BF16)

    o_ref[...] = _dot(h_scr[...], w_ref[...]).astype(BF16)


def _norm_proj(x, g, sc, sh, w_bf16, S):
    T, D = x.shape
    N = w_bf16.shape[1]
    tm, tn = 1024, 1024
    per_batch = S // tm
    return pl.pallas_call(
        _norm_proj_kernel,
        grid=(T // tm, N // tn),
        in_specs=[pl.BlockSpec((tm, D), lambda i, j: (i, 0)),
                  pl.BlockSpec((1, D), lambda i, j: (0, 0)),
                  pl.BlockSpec((1, 1, D), lambda i, j: (i // per_batch, 0, 0)),
                  pl.BlockSpec((1, 1, D), lambda i, j: (i // per_batch, 0, 0)),
                  pl.BlockSpec((D, tn), lambda i, j: (0, j))],
        out_specs=pl.BlockSpec((tm, tn), lambda i, j: (i, j)),
        out_shape=jax.ShapeDtypeStruct((T, N), BF16),
        scratch_shapes=[pltpu.VMEM((tm, D), BF16)],
        compiler_params=_cparams(("arbitrary", "arbitrary")),
        name="norm_proj",
    )(x, g, sc, sh, w_bf16)


def _qk_prep_kernel(q_ref, k_ref, cos_ref, sin_ref, bd_ref, qg_ref, kg_ref, qo_ref, ko_ref):
    cos = cos_ref[...]
    sin = sin_ref[...]
    bd = bd_ref[...]
    lane = lax.broadcasted_iota(jnp.int32, cos.shape, 1)
    first_half = (lane % DA_HEAD_DIM) < (DA_HEAD_DIM // 2)
    half = DA_HEAD_DIM // 2

    def prep(x_ref, g_ref, o_ref, scale):
        x = x_ref[...].astype(F32)
        hi, lo = _split_hi_lo(x * x)
        ms = _dot(hi, bd) + _dot(lo, bd)
        y = x * lax.rsqrt(ms + EPS) * g_ref[...]
        for c in range(DA_WIDTH // LANES):
            yc = y[:, c * LANES:(c + 1) * LANES]
            swapped = jnp.where(first_half, pltpu.roll(yc, LANES - half, 1), pltpu.roll(yc, half, 1))
            o_ref[:, c * LANES:(c + 1) * LANES] = ((yc * cos + swapped * sin) * scale).astype(BF16)

    prep(q_ref, qg_ref, qo_ref, DA_HEAD_DIM ** -0.5)
    prep(k_ref, kg_ref, ko_ref, 1.0)


def _qk_prep(proj, cos, sin, bd, qg, kg):
    T = proj.shape[0]
    tm = 1024
    W = DA_WIDTH
    return pl.pallas_call(
        _qk_prep_kernel,
        grid=(T // tm,),
        in_specs=[pl.BlockSpec((tm, W), lambda i: (i, COL_QA // W)),
                  pl.BlockSpec((tm, W), lambda i: (i, COL_KA // W)),
                  pl.BlockSpec((tm, LANES), lambda i: (i, 0)),
                  pl.BlockSpec((tm, LANES), lambda i: (i, 0)),
                  pl.BlockSpec((W, W), lambda i: (0, 0)),
                  pl.BlockSpec((1, W), lambda i: (0, 0)),
                  pl.BlockSpec((1, W), lambda i: (0, 0))],
        out_specs=[pl.BlockSpec((tm, W), lambda i: (i, 0))] * 2,
        out_shape=[jax.ShapeDtypeStruct((T, W), BF16)] * 2,
        compiler_params=_cparams(("arbitrary",)),
        name="qk_prep",
    )(proj, proj, cos, sin, bd, qg, kg)


def _diff_attn_kernel(shift_ref, lam_ref, q_ref, k_ref, v_ref, g_ref, o_ref, acc1, acc2, *, tq, out_scale, online):
    qi = pl.program_id(2)
    qf = q_ref[...].astype(F32)
    lane = lax.broadcasted_iota(jnp.int32, qf.shape, 1)
    qs = (jnp.where(lane < DA_HEAD_DIM, qf, 0.0).astype(BF16),
          jnp.where(lane >= DA_HEAD_DIM, qf, 0.0).astype(BF16))
    accs = (acc1, acc2)
    acc1[...] = jnp.zeros_like(acc1)
    acc2[...] = jnp.zeros_like(acc2)
    row = lax.broadcasted_iota(jnp.int32, (tq, tq), 0)
    col = lax.broadcasted_iota(jnp.int32, (tq, tq), 1)
    causal = col <= row
    shift = shift_ref[...]

    def step(kj, carry, masked):
        start = pl.multiple_of(kj * tq, tq)
        k = k_ref[pl.ds(start, tq), :]
        v = v_ref[pl.ds(start, tq), :]
        ss = [_dot_nt(q, k) for q in qs]
        if masked:
            ss = [jnp.where(causal, s, NEG_INF) for s in ss]
        if online:
            ms, ls = carry
            m_new = [jnp.maximum(m, jnp.max(s, axis=1, keepdims=True)) for m, s in zip(ms, ss)]
            ps = [jnp.exp(s - m) for s, m in zip(ss, m_new)]
            alphas = [jnp.exp(m - mn) for m, mn in zip(ms, m_new)]
            for acc, alpha, p in zip(accs, alphas, ps):
                acc[...] = alpha * acc[...] + _dot(p.astype(BF16), v)
            ls = [alpha * l + jnp.sum(p, axis=1, keepdims=True) for alpha, l, p in zip(alphas, ls, ps)]
            return tuple(m_new), tuple(ls)
        ps = [jnp.exp(s - shift) for s in ss]
        for acc, p in zip(accs, ps):
            acc[...] += _dot(p.astype(BF16), v)
        return tuple(l + jnp.sum(p, axis=1, keepdims=True) for l, p in zip(carry, ps))

    zero = jnp.zeros((tq, 1), F32)
    if online:
        neg = jnp.full((tq, 1), NEG_INF, F32)
        init = ((neg, neg), (zero, zero))
    else:
        init = (zero, zero)
    carry = lax.fori_loop(0, qi, lambda kj, c: step(kj, c, False), init)
    carry = step(qi, carry, True)
    l1, l2 = carry[1] if online else carry
    o = acc1[...] / l1 - lam_ref[...] * (acc2[...] / l2)
    ms = jnp.mean(o * o, axis=-1, keepdims=True)
    o_ref[...] = (o * lax.rsqrt(ms + EPS) * g_ref[...] * out_scale).astype(BF16)


def _diff_attn(shift, lam_lanes, q_r, k_r, proj, subln_g, B, S, out_scale, online):
    T = q_r.shape[0]
    tq = 512
    nq = S // tq
    return pl.pallas_call(
        functools.partial(_diff_attn_kernel, tq=tq, out_scale=out_scale, online=online),
        grid=(B, DA_HEADS, nq),
        in_specs=[pl.BlockSpec((1, 1), lambda b, h, i: (0, 0)),
                  pl.BlockSpec((1, LANES), lambda b, h, i: (0, 0)),
                  pl.BlockSpec((tq, LANES), lambda b, h, i: (b * nq + i, h)),
                  pl.BlockSpec((S, LANES), lambda b, h, i: (b, h)),
                  pl.BlockSpec((S, LANES), lambda b, h, i: (b, COL_VA // LANES + h)),
                  pl.BlockSpec((1, LANES), lambda b, h, i: (0, 0))],
        out_specs=pl.BlockSpec((tq, LANES), lambda b, h, i: (b * nq + i, h)),
        out_shape=jax.ShapeDtypeStruct((T, DA_WIDTH), BF16),
        scratch_shapes=[pltpu.VMEM((tq, LANES), F32), pltpu.VMEM((tq, LANES), F32)],
        compiler_params=_cparams(("arbitrary", "arbitrary", "arbitrary")),
        name="diff_attn_online" if online else "diff_attn",
    )(shift, lam_lanes, q_r, k_r, proj, subln_g)


def _sb_attn_kernel(q_ref, k_ref, v_ref, u_ref, o_ref, acc, *, tq):
    qi = pl.program_id(2)
    qf = q_ref[...].astype(F32) * (SB_HEAD_DIM ** -0.5 * LOG2E)
    lane = lax.broadcasted_iota(jnp.int32, qf.shape, 1)
    qs = (jnp.where(lane < SB_HEAD_DIM, qf, 0.0).astype(BF16),
          jnp.where(lane >= SB_HEAD_DIM, qf, 0.0).astype(BF16))
    uu = u_ref[...]
    acc[...] = jnp.zeros_like(acc)
    row = lax.broadcasted_iota(jnp.int32, (tq, tq), 0)
    col = lax.broadcasted_iota(jnp.int32, (tq, tq), 1)
    before = col < row

    def step(kjs, rs, masked):
        ks = [k_ref[pl.ds(pl.multiple_of(kj * tq, tq), tq), :] for kj in kjs]
        vs = [v_ref[pl.ds(pl.multiple_of(kj * tq, tq), tq), :] for kj in kjs]
        chains = [(h, b) for b in range(len(kjs)) for h in range(2)]
        zs = [_dot_nt(qs[h], ks[b]) for h, b in chains]
        ps, hls = [], []
        for z in zs:
            neg_abs = pltpu.bitcast(pltpu.bitcast(z, jnp.uint32) | jnp.uint32(0x80000000), F32)
            p = jnp.maximum(z, 0.0) + jnp.log2(1.0 + jnp.exp2(neg_abs))
            if masked:
                p = jnp.where(before, p, 0.0)
            hi, lo = _split_hi_lo(p)
            ps.append(jnp.sum(p, axis=1, keepdims=True))
            hls.append(jnp.concatenate([hi, lo], axis=1))
        mms = [_dot(hl, uu) for hl in hls]
        rs = list(rs)
        for c, (h, b) in enumerate(chains):
            a = jnp.exp2(zs[c] + mms[c] + rs[h])
            if masked:
                a = jnp.where(before, a, 0.0)
            acc[h * tq:(h + 1) * tq, :] += _dot(a.astype(BF16), vs[b])
            rs[h] = rs[h] - ps[c]
        return tuple(rs)

    zero = jnp.zeros((tq, 1), F32)
    rs = step([qi], (zero, zero), True)
    def run(first, count, c):
        return step([first - d for d in range(count)], c, False)

    nb = SB_BLOCKS_PER_TRIP
    rs = lax.fori_loop(0, qi // nb, lambda t, c: run(qi - 1 - nb * t, nb, c), rs)
    left = qi % nb
    part = nb // 2
    while part >= 1:
        rs = lax.cond(left & part != 0, functools.partial(run, (left & (2 * part - 1)) - 1, part),
                      lambda c: c, rs)
        part //= 2

    o_ref[...] = jnp.where(lane < SB_HEAD_DIM, acc[0:tq, :], acc[tq:2 * tq, :]).astype(BF16)


def _sb_attn(proj, u, B, S):
    T = proj.shape[0]
    tq = SB_BLOCK
    nq = S // tq
    return pl.pallas_call(
        functools.partial(_sb_attn_kernel, tq=tq),
        grid=(B, SB_WIDTH // LANES, nq),
        in_specs=[pl.BlockSpec((tq, LANES), lambda b, p, i: (b * nq + i, COL_QC // LANES + p)),
                  pl.BlockSpec((S, LANES), lambda b, p, i: (b, COL_KC // LANES + p)),
                  pl.BlockSpec((S, LANES), lambda b, p, i: (b, COL_VC // LANES + p)),
                  pl.BlockSpec((2 * tq, tq), lambda b, p, i: (0, 0))],
        out_specs=pl.BlockSpec((tq, LANES), lambda b, p, i: (b * nq + i, p)),
        out_shape=jax.ShapeDtypeStruct((T, SB_WIDTH), BF16),
        scratch_shapes=[pltpu.VMEM((2 * tq, LANES), F32)],
        compiler_params=_cparams(("arbitrary", "arbitrary", "arbitrary")),
        name="sb_attn",
    )(proj, proj, proj, u)


def _conv_kernel(a_ref, g_ref, ah_ref, gh_ref, w_ref, b_ref, lg_ref, lb_ref, o_ref, hbuf, *, tm, per_batch, rows):
    i = pl.program_id(0)
    halo = ah_ref[...].astype(F32) * jax.nn.sigmoid(gh_ref[...].astype(F32))
    hbuf[0:CONV_HALO, :] = jnp.where(i % per_batch == 0, 0.0, halo)
    hbuf[CONV_HALO:, :] = a_ref[...].astype(F32) * jax.nn.sigmoid(g_ref[...].astype(F32))
    w = w_ref[...]
    for r0 in range(0, tm, rows):
        acc = jnp.broadcast_to(b_ref[...], (rows, CONV_CH))
        for b in range(SUBLANES):
            part = None
            for a in range((CONV_WIDTH - 1 - b) // SUBLANES + 1):
                lo = r0 + CONV_HALO - SUBLANES - SUBLANES * a
                tap = CONV_WIDTH - 1 - SUBLANES * a - b
                term = hbuf[lo:lo + rows + SUBLANES, :] * w[tap:tap + 1, :]
                part = term if part is None else part + term
            acc = acc + part[SUBLANES - b:SUBLANES - b + rows, :]
        mu = jnp.mean(acc, axis=-1, keepdims=True)
        d = acc - mu
        var = jnp.mean(d * d, axis=-1, keepdims=True)
        y = d * lax.rsqrt(var + EPS) * lg_ref[...] + lb_ref[...]
        o_ref[r0:r0 + rows, :] = (y * jax.nn.sigmoid(y)).astype(BF16)


def _conv_module(proj, w_dw, b_dw, ln_g, ln_b, S):
    T = proj.shape[0]
    tm = 512
    per_batch = S // tm
    C = CONV_CH
    ca, cg = COL_UB // C, COL_UB // C + 1
    halo_blocks = tm // CONV_HALO

    def halo_idx(col):
        return lambda i: (jnp.maximum(i * halo_blocks - 1, 0), col)

    return pl.pallas_call(
        functools.partial(_conv_kernel, tm=tm, per_batch=per_batch, rows=128),
        grid=(T // tm,),
        in_specs=[pl.BlockSpec((tm, C), lambda i: (i, ca)),
                  pl.BlockSpec((tm, C), lambda i: (i, cg)),
                  pl.BlockSpec((CONV_HALO, C), halo_idx(ca)),
                  pl.BlockSpec((CONV_HALO, C), halo_idx(cg)),
                  pl.BlockSpec((CONV_WIDTH, C), lambda i: (0, 0)),
                  pl.BlockSpec((1, C), lambda i: (0, 0)),
                  pl.BlockSpec((1, C), lambda i: (0, 0)),
                  pl.BlockSpec((1, C), lambda i: (0, 0))],
        out_specs=pl.BlockSpec((tm, C), lambda i: (i, 0)),
        out_shape=jax.ShapeDtypeStruct((T, C), BF16),
        scratch_shapes=[pltpu.VMEM((tm + CONV_HALO, C), F32)],
        compiler_params=_cparams(("arbitrary",)),
        name="conv_module",
    )(proj, proj, proj, proj, w_dw, b_dw, ln_g, ln_b)


def _merge_kernel(oa_ref, cb_ref, oc_ref, g0_ref, g1_ref, g2_ref, x_ref, gm_ref,
                  wa_ref, wb_ref, bb_ref, wc_ref, wo_ref, o_ref):
    y_a = _dot(oa_ref[...], wa_ref[...])
    y_b = _dot(cb_ref[...], wb_ref[...]) + bb_ref[...]
    y_c = _dot(oc_ref[...], wc_ref[...])
    merged = (jax.nn.sigmoid(g0_ref[...].astype(F32)) * y_a
              + jax.nn.sigmoid(g1_ref[...].astype(F32)) * y_b
              + jax.nn.sigmoid(g2_ref[...].astype(F32)) * y_c)
    o_ref[...] = x_ref[...] + gm_ref[0] * _dot(merged.astype(BF16), wo_ref[...])


def _merge(o_a, cb, o_c, proj, x, g_m, wa, wb, bb, wc, wo, S):
    T, D = x.shape
    tm = 512
    per_batch = S // tm
    W = DA_WIDTH
    gcol = COL_GATES // D
    branch = pl.BlockSpec((tm, W), lambda i: (i, 0))
    wspec = pl.BlockSpec((W, D), lambda i: (0, 0))
    return pl.pallas_call(
        _merge_kernel,
        grid=(T // tm,),
        in_specs=[branch, branch, branch,
                  pl.BlockSpec((tm, D), lambda i: (i, gcol)),
                  pl.BlockSpec((tm, D), lambda i: (i, gcol + 1)),
                  pl.BlockSpec((tm, D), lambda i: (i, gcol + 2)),
                  pl.BlockSpec((tm, D), lambda i: (i, 0)),
                  pl.BlockSpec((1, 1, D), lambda i: (i // per_batch, 0, 0)),
                  wspec, wspec, pl.BlockSpec((1, D), lambda i: (0, 0)), wspec,
                  pl.BlockSpec((D, D), lambda i: (0, 0))],
        out_specs=pl.BlockSpec((tm, D), lambda i: (i, 0)),
        out_shape=jax.ShapeDtypeStruct((T, D), F32),
        compiler_params=_cparams(("arbitrary",)),
        name="merge",
    )(o_a, cb, o_c, proj, proj, proj, x, g_m, wa, wb, bb, wc, wo)


def _first_index_of_max(vals, idx, size):
    mx = jnp.max(vals, axis=0, keepdims=True)
    first = jnp.min(jnp.where(vals == mx, idx, size), axis=0, keepdims=True)
    return mx, first


def _router_kernel(x_ref, g_ref, sc_ref, sh_ref, wh_ref, wl_ref, b_ref, h_ref, gate_ref):
    h = _ada_rms(x_ref[...], g_ref[...], sc_ref[0], sh_ref[0])
    h_ref[...] = h.astype(BF16)
    h_hi, h_lo = _split_hi_lo(h)
    logits = _dot_nt(wh_ref[...], h_hi) + _dot_nt(wh_ref[...], h_lo) + _dot_nt(wl_ref[...], h_hi)
    scores = jax.nn.sigmoid(logits)
    biased = scores + b_ref[...]
    tm = scores.shape[1]
    minus_inf = -jnp.inf

    in_group = lax.broadcasted_iota(jnp.int32, (GROUP_SIZE, tm), 0)
    group_scores = []
    for g in range(N_GROUPS):
        vals = biased[g * GROUP_SIZE:(g + 1) * GROUP_SIZE]
        m1, i1 = _first_index_of_max(vals, in_group, GROUP_SIZE)
        m2 = jnp.max(jnp.where(in_group == i1, minus_inf, vals), axis=0, keepdims=True)
        group_scores.append(m1 + m2)
    gs = jnp.concatenate(group_scores, axis=0)

    gidx = lax.broadcasted_iota(jnp.int32, (N_GROUPS, tm), 0)
    group_sel = jnp.zeros((N_GROUPS, tm), jnp.bool_)
    for _ in range(TOPK_GROUPS):
        _, first = _first_index_of_max(gs, gidx, N_GROUPS)
        pick = gidx == first
        group_sel = jnp.logical_or(group_sel, pick)
        gs = jnp.where(pick, minus_inf, gs)

    eidx = lax.broadcasted_iota(jnp.int32, (N_EXPERTS, tm), 0)
    expert_group_sel = jnp.concatenate(
        [jnp.broadcast_to(group_sel[g:g + 1], (GROUP_SIZE, tm)) for g in range(N_GROUPS)], axis=0)
    cand = jnp.where(expert_group_sel, biased, NEG_INF)
    chosen = jnp.zeros((N_EXPERTS, tm), jnp.bool_)
    for _ in range(TOP_K):
        _, first = _first_index_of_max(cand, eidx, N_EXPERTS)
        pick = eidx == first
        chosen = jnp.logical_or(chosen, pick)
        cand = jnp.where(pick, minus_inf, cand)

    w = jnp.where(chosen, scores, 0.0)
    w = w / (jnp.sum(w, axis=0, keepdims=True) + 1e-20) * ROUTED_SCALE
    gate_ref[...] = w


def _router(x1, g, sc, sh, wr_hi, wr_lo, b_router, S):
    T, D = x1.shape
    tm = 512
    per_batch = S // tm
    E = N_EXPERTS
    return pl.pallas_call(
        _router_kernel,
        grid=(T // tm,),
        in_specs=[pl.BlockSpec((tm, D), lambda i: (i, 0)),
                  pl.BlockSpec((1, D), lambda i: (0, 0)),
                  pl.BlockSpec((1, 1, D), lambda i: (i // per_batch, 0, 0)),
                  pl.BlockSpec((1, 1, D), lambda i: (i // per_batch, 0, 0)),
                  pl.BlockSpec((E, D), lambda i: (0, 0)),
                  pl.BlockSpec((E, D), lambda i: (0, 0)),
                  pl.BlockSpec((E, 1), lambda i: (0, 0))],
        out_specs=[pl.BlockSpec((tm, D), lambda i: (i, 0)),
                   pl.BlockSpec((E, tm), lambda i: (0, i))],
        out_shape=[jax.ShapeDtypeStruct((T, D), BF16),
                   jax.ShapeDtypeStruct((E, T), F32)],
        compiler_params=_cparams(("arbitrary",)),
        name="router",
    )(x1, g, sc, sh, wr_hi, wr_lo, b_router)


def _moe_kernel(h_ref, gate_ref, x_ref, gf_ref, w1_ref, w3_ref, w2_ref, o_ref, acc, *, chunk):
    c = pl.program_id(1)

    @pl.when(c == 0)
    def _():
        acc[...] = jnp.zeros_like(acc)

    h = h_ref[...]
    gates = gate_ref[...]
    lane = lax.broadcasted_iota(jnp.int32, gates.shape, 1)
    total = jnp.zeros(acc.shape, F32)
    for k in range(chunk):
        e = c * chunk + k
        gcol = jnp.sum(jnp.where(lane == e, gates, 0.0), axis=1, keepdims=True)
        a = _dot(h, w1_ref[k])
        hid = a * jax.nn.sigmoid(a) * _dot(h, w3_ref[k]) * gcol
        total = total + _dot(hid.astype(BF16), w2_ref[k])
    acc[...] += total

    @pl.when(c == pl.num_programs(1) - 1)
    def _():
        o_ref[...] = x_ref[...] + gf_ref[0] * acc[...]


def _moe(h2, gates_tok, x1, g_f, w1, w3, w2, S):
    T, D = x1.shape
    n_exp = w1.shape[0]
    chunk = 5
    tm = 1024
    per_batch = S // tm
    F = EXPERT_FF
    return pl.pallas_call(
        functools.partial(_moe_kernel, chunk=chunk),
        grid=(T // tm, n_exp // chunk),
        in_specs=[pl.BlockSpec((tm, D), lambda i, c: (i, 0)),
                  pl.BlockSpec((tm, LANES), lambda i, c: (i, 0)),
                  pl.BlockSpec((tm, D), lambda i, c: (i, 0)),
                  pl.BlockSpec((1, 1, D), lambda i, c: (i // per_batch, 0, 0)),
                  pl.BlockSpec((chunk, D, F), lambda i, c: (c, 0, 0)),
                  pl.BlockSpec((chunk, D, F), lambda i, c: (c, 0, 0)),
                  pl.BlockSpec((chunk, F, D), lambda i, c: (c, 0, 0))],
        out_specs=pl.BlockSpec((tm, D), lambda i, c: (i, 0)),
        out_shape=jax.ShapeDtypeStruct((T, D), F32),
        scratch_shapes=[pltpu.VMEM((tm, D), F32)],
        compiler_params=_cparams(("arbitrary", "arbitrary")),
        name="moe_experts",
    )(h2, gates_tok, x1, g_f, w1, w3, w2)


def kernel(x, c, positions, w_mod, b_mod, norm_mix_g, norm_ffn_g, w_in, qn_g, kn_g, lam_q1, lam_k1, lam_q2, lam_k2, subln_g, w_proj_a, w_dw, b_dw, conv_ln_g, conv_ln_b, w_proj_b, b_proj_b, w_proj_c, w_out, w_router, b_router, w1, w3, w2, ws1, ws3, ws2):
    B, S, D = x.shape
    T = B * S
    depth = w_mod.shape[0]
    xt = x.reshape(T, D)

    inv = ROPE_THETA ** (-jnp.arange(0, DA_HEAD_DIM, 2, dtype=F32) / DA_HEAD_DIM)
    inv_lanes = jnp.tile(inv, LANES // (DA_HEAD_DIM // 2)).reshape(1, LANES)
    pos_lanes = jnp.broadcast_to(positions.astype(F32).reshape(T, 1), (T, LANES))
    cos, sin = _rope_tables(pos_lanes, inv_lanes)

    mod = _modulation(c, w_mod, b_mod)

    seg = jnp.arange(DA_WIDTH, dtype=jnp.int32) // DA_HEAD_DIM
    bd = jnp.where(seg[:, None] == seg[None, :], 1.0 / DA_HEAD_DIM, 0.0).astype(BF16)
    kk = jnp.arange(SB_BLOCK, dtype=jnp.int32)
    u = jnp.where(kk[:, None] >= kk[None, :], -1.0, 0.0).astype(BF16)
    u = jnp.concatenate([u, u], axis=0)
    n_seg = DA_WIDTH // DA_HEAD_DIM

    for l in range(depth):
        lambda_init = 0.8 - 0.6 * math.exp(-0.3 * l)
        sh_m, sc_m, g_m, sh_f, sc_f, g_f = [m.reshape(B, 1, D) for m in jnp.split(mod[l], 6, axis=-1)]

        proj = _norm_proj(xt, norm_mix_g[l].reshape(1, D), sc_m, sh_m, w_in[l].astype(BF16), S)

        q_r, k_r = _qk_prep(proj, cos, sin, bd,
                            jnp.tile(qn_g[l], n_seg).reshape(1, DA_WIDTH),
                            jnp.tile(kn_g[l], n_seg).reshape(1, DA_WIDTH))
        lam = (jnp.exp(jnp.sum(lam_q1[l] * lam_k1[l])) - jnp.exp(jnp.sum(lam_q2[l] * lam_k2[l])) + lambda_init)
        score_bound = (DA_HEAD_DIM ** 0.5) * jnp.max(jnp.abs(qn_g[l])) * jnp.max(jnp.abs(kn_g[l]))
        attn = functools.partial(_diff_attn, B=B, S=S, out_scale=1.0 - lambda_init)
        o_a = lax.cond(score_bound <= MAX_FIXED_SHIFT,
                       functools.partial(attn, online=False), functools.partial(attn, online=True),
                       score_bound.reshape(1, 1).astype(F32), jnp.full((1, LANES), lam, F32), q_r, k_r, proj,
                       subln_g[l].reshape(1, LANES))
        cb = _conv_module(proj, w_dw[l], b_dw[l].reshape(1, CONV_CH), conv_ln_g[l].reshape(1, CONV_CH),
                          conv_ln_b[l].reshape(1, CONV_CH), S)
        o_c = _sb_attn(proj, u, B, S)
        x1 = _merge(o_a, cb, o_c, proj, xt, g_m, w_proj_a[l].astype(BF16), w_proj_b[l].astype(BF16),
                    b_proj_b[l].reshape(1, D), w_proj_c[l].astype(BF16), w_out[l].astype(BF16), S)

        wr_t = w_router[l].T
        wr_hi = wr_t.astype(BF16)
        wr_lo = (wr_t - wr_hi.astype(F32)).astype(BF16)
        h2, gates = _router(x1, norm_ffn_g[l].reshape(1, D), sc_f, sh_f, wr_hi, wr_lo,
                            b_router[l].reshape(N_EXPERTS, 1), S)
        gates_tok = jnp.concatenate(
            [gates.T, jnp.ones((T, 1), F32), jnp.zeros((T, LANES - N_EXPERTS - 1), F32)], axis=1)
        w1_all = jnp.concatenate([w1[l], ws1[l][None]], axis=0).astype(BF16)
        w3_all = jnp.concatenate([w3[l], ws3[l][None]], axis=0).astype(BF16)
        w2_all = jnp.concatenate([w2[l], ws2[l][None]], axis=0).astype(BF16)
        xt = _moe(h2, gates_tok, x1, g_f, w1_all, w3_all, w2_all, S)

    return xt.reshape(B, S, D)
```

```python
import functools
import math

import jax
import jax.numpy as jnp
from jax import lax
from jax.experimental import pallas as pl
from jax.experimental.pallas import tpu as pltpu

F32 = jnp.float32
BF16 = jnp.bfloat16

D_MODEL = 1024
DA_HEADS = 4
DA_HEAD_DIM = 64
DA_WIDTH = 512
CONV_CH = 512
CONV_WIDTH = 31
SB_WIDTH = 512
SB_HEAD_DIM = 64
IN_COLS = 7168
ROPE_THETA = 10000.0
EPS = 1e-6
NEG_INF = -1e30
LOG2E = math.log2(math.e)
MAX_FIXED_SHIFT = 40.0
SB_BLOCK = 256
SB_BLOCKS_PER_TRIP = 4
N_EXPERTS = 64
TOP_K = 8
N_GROUPS = 8
TOPK_GROUPS = 4
GROUP_SIZE = N_EXPERTS // N_GROUPS
EXPERT_FF = 256
ROUTED_SCALE = 2.5

MOE_TILE = 512
SEG_ALIGN = 16
SEG_PIECES = tuple(SEG_ALIGN << j for j in reversed(range((MOE_TILE // SEG_ALIGN).bit_length())))
LOCAL_SLOTS = -(-(MOE_TILE * TOP_K + N_EXPERTS * (SEG_ALIGN - 1)) // 512) * 512
EXPERT_BLOCK_ROWS = 512
LANES = 128
SUBLANES = 8
CONV_HALO = 32

COL_QA, COL_KA, COL_VA, COL_UB, COL_QC, COL_KC, COL_VC, COL_GATES = 0, 512, 1024, 1536, 2560, 3072, 3584, 4096

VMEM_LIMIT = 48 * 1024 * 1024


def _cparams(sem):
    return pltpu.CompilerParams(dimension_semantics=sem, vmem_limit_bytes=VMEM_LIMIT)


def _dot(a, b):
    return jnp.dot(a, b, preferred_element_type=F32)


def _dot_nt(a, b):
    return lax.dot_general(a, b, (((1,), (1,)), ((), ())), preferred_element_type=F32)


def _split_hi_lo(x):
    hi = x.astype(BF16)
    lo = (x - hi.astype(F32)).astype(BF16)
    return hi, lo


def _mod_kernel(c_ref, w_ref, b_ref, o_ref):
    c = c_ref[...]
    c_act = c * jax.nn.sigmoid(c)
    o_ref[0] = jnp.dot(c_act, w_ref[0], precision=lax.Precision.HIGHEST,
                       preferred_element_type=F32) + b_ref[0]


def _modulation(c, w_mod, b_mod):
    L, D, N = w_mod.shape
    B = c.shape[0]
    tn = 1536
    return pl.pallas_call(
        _mod_kernel,
        grid=(L, N // tn),
        in_specs=[pl.BlockSpec((B, D), lambda l, j: (0, 0)),
                  pl.BlockSpec((1, D, tn), lambda l, j: (l, 0, j)),
                  pl.BlockSpec((1, 1, tn), lambda l, j: (l, 0, j))],
        out_specs=pl.BlockSpec((1, B, tn), lambda l, j: (l, 0, j)),
        out_shape=jax.ShapeDtypeStruct((L, B, N), F32),
        compiler_params=_cparams(("arbitrary", "arbitrary")),
        name="modulation",
    )(c, w_mod, b_mod.reshape(L, 1, N))


def _rope_kernel(pos_ref, inv_ref, cos_ref, sin_ref):
    ang = pos_ref[...] * inv_ref[...]
    lane = lax.broadcasted_iota(jnp.int32, ang.shape, 1)
    first_half = (lane % DA_HEAD_DIM) < (DA_HEAD_DIM // 2)
    cos_ref[...] = jnp.cos(ang)
    s = jnp.sin(ang)
    sin_ref[...] = jnp.where(first_half, -s, s)


def _rope_tables(pos_lanes, inv_lanes):
    T = pos_lanes.shape[0]
    tm = 2048
    return pl.pallas_call(
        _rope_kernel,
        grid=(T // tm,),
        in_specs=[pl.BlockSpec((tm, LANES), lambda i: (i, 0)),
                  pl.BlockSpec((1, LANES), lambda i: (0, 0))],
        out_specs=[pl.BlockSpec((tm, LANES), lambda i: (i, 0))] * 2,
        out_shape=[jax.ShapeDtypeStruct((T, LANES), F32)] * 2,
        compiler_params=_cparams(("arbitrary",)),
        name="rope_tables",
    )(pos_lanes, inv_lanes)


def _ada_rms(x, g, sc, sh):
    ms = jnp.mean(x * x, axis=-1, keepdims=True)
    return x * lax.rsqrt(ms + EPS) * g * (1.0 + sc) + sh


def _norm_proj_kernel(x_ref, g_ref, sc_ref, sh_ref, w_ref, o_ref, h_scr):
    @pl.when(pl.program_id(1) == 0)
    def _():
        h_scr[...] = _ada_rms(x_ref[...], g_ref[...], sc_ref[0], sh_ref[0]).astype(BF16)

    o_ref[...] = _dot(h_scr[...], w_ref[...]).astype(BF16)


def _norm_proj(x, g, sc, sh, w_bf16, S):
    T, D = x.shape
    N = w_bf16.shape[1]
    tm, tn = 1024, 1024
    per_batch = S // tm
    return pl.pallas_call(
        _norm_proj_kernel,
        grid=(T // tm, N // tn),
        in_specs=[pl.BlockSpec((tm, D), lambda i, j: (i, 0)),
                  pl.BlockSpec((1, D), lambda i, j: (0, 0)),
                  pl.BlockSpec((1, 1, D), lambda i, j: (i // per_batch, 0, 0)),
                  pl.BlockSpec((1, 1, D), lambda i, j: (i // per_batch, 0, 0)),
                  pl.BlockSpec((D, tn), lambda i, j: (0, j))],
        out_specs=pl.BlockSpec((tm, tn), lambda i, j: (i, j)),
        out_shape=jax.ShapeDtypeStruct((T, N), BF16),
        scratch_shapes=[pltpu.VMEM((tm, D), BF16)],
        compiler_params=_cparams(("arbitrary", "arbitrary")),
        name="norm_proj",
    )(x, g, sc, sh, w_bf16)


def _qk_prep_kernel(q_ref, k_ref, cos_ref, sin_ref, bd_ref, qg_ref, kg_ref, qo_ref, ko_ref):
    cos = cos_ref[...]
    sin = sin_ref[...]
    bd = bd_ref[...]
    lane = lax.broadcasted_iota(jnp.int32, cos.shape, 1)
    first_half = (lane % DA_HEAD_DIM) < (DA_HEAD_DIM // 2)
    half = DA_HEAD_DIM // 2

    def prep(x_ref, g_ref, o_ref, scale):
        x = x_ref[...].astype(F32)
        hi, lo = _split_hi_lo(x * x)
        ms = _dot(hi, bd) + _dot(lo, bd)
        y = x * lax.rsqrt(ms + EPS) * g_ref[...]
        for c in range(DA_WIDTH // LANES):
            yc = y[:, c * LANES:(c + 1) * LANES]
            swapped = jnp.where(first_half, pltpu.roll(yc, LANES - half, 1), pltpu.roll(yc, half, 1))
            o_ref[:, c * LANES:(c + 1) * LANES] = ((yc * cos + swapped * sin) * scale).astype(BF16)

    prep(q_ref, qg_ref, qo_ref, DA_HEAD_DIM ** -0.5)
    prep(k_ref, kg_ref, ko_ref, 1.0)


def _qk_prep(proj, cos, sin, bd, qg, kg):
    T = proj.shape[0]
    tm = 1024
    W = DA_WIDTH
    return pl.pallas_call(
        _qk_prep_kernel,
        grid=(T // tm,),
        in_specs=[pl.BlockSpec((tm, W), lambda i: (i, COL_QA // W)),
                  pl.BlockSpec((tm, W), lambda i: (i, COL_KA // W)),
                  pl.BlockSpec((tm, LANES), lambda i: (i, 0)),
                  pl.BlockSpec((tm, LANES), lambda i: (i, 0)),
                  pl.BlockSpec((W, W), lambda i: (0, 0)),
                  pl.BlockSpec((1, W), lambda i: (0, 0)),
                  pl.BlockSpec((1, W), lambda i: (0, 0))],
        out_specs=[pl.BlockSpec((tm, W), lambda i: (i, 0))] * 2,
        out_shape=[jax.ShapeDtypeStruct((T, W), BF16)] * 2,
        compiler_params=_cparams(("arbitrary",)),
        name="qk_prep",
    )(proj, proj, cos, sin, bd, qg, kg)


def _diff_attn_kernel(shift_ref, lam_ref, q_ref, k_ref, v_ref, g_ref, o_ref, acc1, acc2, *, tq, out_scale, online):
    qi = pl.program_id(2)
    qf = q_ref[...].astype(F32)
    lane = lax.broadcasted_iota(jnp.int32, qf.shape, 1)
    qs = (jnp.where(lane < DA_HEAD_DIM, qf, 0.0).astype(BF16),
          jnp.where(lane >= DA_HEAD_DIM, qf, 0.0).astype(BF16))
    accs = (acc1, acc2)
    acc1[...] = jnp.zeros_like(acc1)
    acc2[...] = jnp.zeros_like(acc2)
    row = lax.broadcasted_iota(jnp.int32, (tq, tq), 0)
    col = lax.broadcasted_iota(jnp.int32, (tq, tq), 1)
    causal = col <= row
    shift = shift_ref[...]

    def step(kj, carry, masked):
        start = pl.multiple_of(kj * tq, tq)
        k = k_ref[pl.ds(start, tq), :]
        v = v_ref[pl.ds(start, tq), :]
        ss = [_dot_nt(q, k) for q in qs]
        if masked:
            ss = [jnp.where(causal, s, NEG_INF) for s in ss]
        if online:
            ms, ls = carry
            m_new = [jnp.maximum(m, jnp.max(s, axis=1, keepdims=True)) for m, s in zip(ms, ss)]
            ps = [jnp.exp(s - m) for s, m in zip(ss, m_new)]
            alphas = [jnp.exp(m - mn) for m, mn in zip(ms, m_new)]
            for acc, alpha, p in zip(accs, alphas, ps):
                acc[...] = alpha * acc[...] + _dot(p.astype(BF16), v)
            ls = [alpha * l + jnp.sum(p, axis=1, keepdims=True) for alpha, l, p in zip(alphas, ls, ps)]
            return tuple(m_new), tuple(ls)
        ps = [jnp.exp(s - shift) for s in ss]
        for acc, p in zip(accs, ps):
            acc[...] += _dot(p.astype(BF16), v)
        return tuple(l + jnp.sum(p, axis=1, keepdims=True) for l, p in zip(carry, ps))

    zero = jnp.zeros((tq, 1), F32)
    if online:
        neg = jnp.full((tq, 1), NEG_INF, F32)
        init = ((neg, neg), (zero, zero))
    else:
        init = (zero, zero)
    carry = lax.fori_loop(0, qi, lambda kj, c: step(kj, c, False), init)
    carry = step(qi, carry, True)
    l1, l2 = carry[1] if online else carry
    o = acc1[...] / l1 - lam_ref[...] * (acc2[...] / l2)
    ms = jnp.mean(o * o, axis=-1, keepdims=True)
    o_ref[...] = (o * lax.rsqrt(ms + EPS) * g_ref[...] * out_scale).astype(BF16)


def _diff_attn(shift, lam_lanes, q_r, k_r, proj, subln_g, B, S, out_scale, online):
    T = q_r.shape[0]
    tq = 512
    nq = S // tq
    return pl.pallas_call(
        functools.partial(_diff_attn_kernel, tq=tq, out_scale=out_scale, online=online),
        grid=(B, DA_HEADS, nq),
        in_specs=[pl.BlockSpec((1, 1), lambda b, h, i: (0, 0)),
                  pl.BlockSpec((1, LANES), lambda b, h, i: (0, 0)),
                  pl.BlockSpec((tq, LANES), lambda b, h, i: (b * nq + i, h)),
                  pl.BlockSpec((S, LANES), lambda b, h, i: (b, h)),
                  pl.BlockSpec((S, LANES), lambda b, h, i: (b, COL_VA // LANES + h)),
                  pl.BlockSpec((1, LANES), lambda b, h, i: (0, 0))],
        out_specs=pl.BlockSpec((tq, LANES), lambda b, h, i: (b * nq + i, h)),
        out_shape=jax.ShapeDtypeStruct((T, DA_WIDTH), BF16),
        scratch_shapes=[pltpu.VMEM((tq, LANES), F32), pltpu.VMEM((tq, LANES), F32)],
        compiler_params=_cparams(("arbitrary", "arbitrary", "arbitrary")),
        name="diff_attn_online" if online else "diff_attn",
    )(shift, lam_lanes, q_r, k_r, proj, subln_g)


def _sb_attn_kernel(q_ref, k_ref, v_ref, u_ref, o_ref, acc, *, tq):
    qi = pl.program_id(2)
    qf = q_ref[...].astype(F32) * (SB_HEAD_DIM ** -0.5 * LOG2E)
    lane = lax.broadcasted_iota(jnp.int32, qf.shape, 1)
    qs = (jnp.where(lane < SB_HEAD_DIM, qf, 0.0).astype(BF16),
          jnp.where(lane >= SB_HEAD_DIM, qf, 0.0).astype(BF16))
    uu = u_ref[...]
    acc[...] = jnp.zeros_like(acc)
    row = lax.broadcasted_iota(jnp.int32, (tq, tq), 0)
    col = lax.broadcasted_iota(jnp.int32, (tq, tq), 1)
    before = col < row

    def step(kjs, rs, masked):
        ks = [k_ref[pl.ds(pl.multiple_of(kj * tq, tq), tq), :] for kj in kjs]
        vs = [v_ref[pl.ds(pl.multiple_of(kj * tq, tq), tq), :] for kj in kjs]
        chains = [(h, b) for b in range(len(kjs)) for h in range(2)]
        zs = [_dot_nt(qs[h], ks[b]) for h, b in chains]
        ps, hls = [], []
        for z in zs:
            neg_abs = pltpu.bitcast(pltpu.bitcast(z, jnp.uint32) | jnp.uint32(0x80000000), F32)
            p = jnp.maximum(z, 0.0) + jnp.log2(1.0 + jnp.exp2(neg_abs))
            if masked:
                p = jnp.where(before, p, 0.0)
            hi, lo = _split_hi_lo(p)
            ps.append(jnp.sum(p, axis=1, keepdims=True))
            hls.append(jnp.concatenate([hi, lo], axis=1))
        mms = [_dot(hl, uu) for hl in hls]
        rs = list(rs)
        for c, (h, b) in enumerate(chains):
            a = jnp.exp2(zs[c] + mms[c] + rs[h])
            if masked:
                a = jnp.where(before, a, 0.0)
            acc[h * tq:(h + 1) * tq, :] += _dot(a.astype(BF16), vs[b])
            rs[h] = rs[h] - ps[c]
        return tuple(rs)

    zero = jnp.zeros((tq, 1), F32)
    rs = step([qi], (zero, zero), True)
    def run(first, count, c):
        return step([first - d for d in range(count)], c, False)

    nb = SB_BLOCKS_PER_TRIP
    rs = lax.fori_loop(0, qi // nb, lambda t, c: run(qi - 1 - nb * t, nb, c), rs)
    left = qi % nb
    part = nb // 2
    while part >= 1:
        rs = lax.cond(left & part != 0, functools.partial(run, (left & (2 * part - 1)) - 1, part),
                      lambda c: c, rs)
        part //= 2

    o_ref[...] = jnp.where(lane < SB_HEAD_DIM, acc[0:tq, :], acc[tq:2 * tq, :]).astype(BF16)


def _sb_attn(proj, u, B, S):
    T = proj.shape[0]
    tq = SB_BLOCK
    nq = S // tq
    return pl.pallas_call(
        functools.partial(_sb_attn_kernel, tq=tq),
        grid=(B, SB_WIDTH // LANES, nq),
        in_specs=[pl.BlockSpec((tq, LANES), lambda b, p, i: (b * nq + i, COL_QC // LANES + p)),
                  pl.BlockSpec((S, LANES), lambda b, p, i: (b, COL_KC // LANES + p)),
                  pl.BlockSpec((S, LANES), lambda b, p, i: (b, COL_VC // LANES + p)),
                  pl.BlockSpec((2 * tq, tq), lambda b, p, i: (0, 0))],
        out_specs=pl.BlockSpec((tq, LANES), lambda b, p, i: (b * nq + i, p)),
        out_shape=jax.ShapeDtypeStruct((T, SB_WIDTH), BF16),
        scratch_shapes=[pltpu.VMEM((2 * tq, LANES), F32)],
        compiler_params=_cparams(("arbitrary", "arbitrary", "arbitrary")),
        name="sb_attn",
    )(proj, proj, proj, u)


def _conv_kernel(a_ref, g_ref, ah_ref, gh_ref, w_ref, b_ref, lg_ref, lb_ref, o_ref, hbuf, *, tm, per_batch, rows):
    i = pl.program_id(0)
    halo = ah_ref[...].astype(F32) * jax.nn.sigmoid(gh_ref[...].astype(F32))
    hbuf[0:CONV_HALO, :] = jnp.where(i % per_batch == 0, 0.0, halo)
    hbuf[CONV_HALO:, :] = a_ref[...].astype(F32) * jax.nn.sigmoid(g_ref[...].astype(F32))
    w = w_ref[...]
    for r0 in range(0, tm, rows):
        acc = jnp.broadcast_to(b_ref[...], (rows, CONV_CH))
        for b in range(SUBLANES):
            part = None
            for a in range((CONV_WIDTH - 1 - b) // SUBLANES + 1):
                lo = r0 + CONV_HALO - SUBLANES - SUBLANES * a
                tap = CONV_WIDTH - 1 - SUBLANES * a - b
                term = hbuf[lo:lo + rows + SUBLANES, :] * w[tap:tap + 1, :]
                part = term if part is None else part + term
            acc = acc + part[SUBLANES - b:SUBLANES - b + rows, :]
        mu = jnp.mean(acc, axis=-1, keepdims=True)
        d = acc - mu
        var = jnp.mean(d * d, axis=-1, keepdims=True)
        y = d * lax.rsqrt(var + EPS) * lg_ref[...] + lb_ref[...]
        o_ref[r0:r0 + rows, :] = (y * jax.nn.sigmoid(y)).astype(BF16)


def _conv_module(proj, w_dw, b_dw, ln_g, ln_b, S):
    T = proj.shape[0]
    tm = 512
    per_batch = S // tm
    C = CONV_CH
    ca, cg = COL_UB // C, COL_UB // C + 1
    halo_blocks = tm // CONV_HALO

    def halo_idx(col):
        return lambda i: (jnp.maximum(i * halo_blocks - 1, 0), col)

    return pl.pallas_call(
        functools.partial(_conv_kernel, tm=tm, per_batch=per_batch, rows=128),
        grid=(T // tm,),
        in_specs=[pl.BlockSpec((tm, C), lambda i: (i, ca)),
                  pl.BlockSpec((tm, C), lambda i: (i, cg)),
                  pl.BlockSpec((CONV_HALO, C), halo_idx(ca)),
                  pl.BlockSpec((CONV_HALO, C), halo_idx(cg)),
                  pl.BlockSpec((CONV_WIDTH, C), lambda i: (0, 0)),
                  pl.BlockSpec((1, C), lambda i: (0, 0)),
                  pl.BlockSpec((1, C), lambda i: (0, 0)),
                  pl.BlockSpec((1, C), lambda i: (0, 0))],
        out_specs=pl.BlockSpec((tm, C), lambda i: (i, 0)),
        out_shape=jax.ShapeDtypeStruct((T, C), BF16),
        scratch_shapes=[pltpu.VMEM((tm + CONV_HALO, C), F32)],
        compiler_params=_cparams(("arbitrary",)),
        name="conv_module",
    )(proj, proj, proj, proj, w_dw, b_dw, ln_g, ln_b)


def _merge_kernel(oa_ref, cb_ref, oc_ref, g0_ref, g1_ref, g2_ref, x_ref, gm_ref,
                  wa_ref, wb_ref, bb_ref, wc_ref, wo_ref, o_ref):
    y_a = _dot(oa_ref[...], wa_ref[...])
    y_b = _dot(cb_ref[...], wb_ref[...]) + bb_ref[...]
    y_c = _dot(oc_ref[...], wc_ref[...])
    merged = (jax.nn.sigmoid(g0_ref[...].astype(F32)) * y_a
              + jax.nn.sigmoid(g1_ref[...].astype(F32)) * y_b
              + jax.nn.sigmoid(g2_ref[...].astype(F32)) * y_c)
    o_ref[...] = x_ref[...] + gm_ref[0] * _dot(merged.astype(BF16), wo_ref[...])


def _merge(o_a, cb, o_c, proj, x, g_m, wa, wb, bb, wc, wo, S):
    T, D = x.shape
    tm = 512
    per_batch = S // tm
    W = DA_WIDTH
    gcol = COL_GATES // D
    branch = pl.BlockSpec((tm, W), lambda i: (i, 0))
    wspec = pl.BlockSpec((W, D), lambda i: (0, 0))
    return pl.pallas_call(
        _merge_kernel,
        grid=(T // tm,),
        in_specs=[branch, branch, branch,
                  pl.BlockSpec((tm, D), lambda i: (i, gcol)),
                  pl.BlockSpec((tm, D), lambda i: (i, gcol + 1)),
                  pl.BlockSpec((tm, D), lambda i: (i, gcol + 2)),
                  pl.BlockSpec((tm, D), lambda i: (i, 0)),
                  pl.BlockSpec((1, 1, D), lambda i: (i // per_batch, 0, 0)),
                  wspec, wspec, pl.BlockSpec((1, D), lambda i: (0, 0)), wspec,
                  pl.BlockSpec((D, D), lambda i: (0, 0))],
        out_specs=pl.BlockSpec((tm, D), lambda i: (i, 0)),
        out_shape=jax.ShapeDtypeStruct((T, D), F32),
        compiler_params=_cparams(("arbitrary",)),
        name="merge",
    )(o_a, cb, o_c, proj, proj, proj, x, g_m, wa, wb, bb, wc, wo)


def _first_index_of_max(vals, idx, size):
    mx = jnp.max(vals, axis=0, keepdims=True)
    first = jnp.min(jnp.where(vals == mx, idx, size), axis=0, keepdims=True)
    return mx, first


def _router_kernel(x_ref, g_ref, sc_ref, sh_ref, wh_ref, wl_ref, b_ref, before_ref, lower_ref,
                   h_ref, slot_ref, gate_ref, count_ref):
    h = _ada_rms(x_ref[...], g_ref[...], sc_ref[0], sh_ref[0])
    h_ref[...] = h.astype(BF16)
    h_hi, h_lo = _split_hi_lo(h)
    logits = _dot_nt(wh_ref[...], h_hi) + _dot_nt(wh_ref[...], h_lo) + _dot_nt(wl_ref[...], h_hi)
    scores = jax.nn.sigmoid(logits)
    biased = scores + b_ref[...]
    tm = scores.shape[1]
    minus_inf = -jnp.inf

    in_group = lax.broadcasted_iota(jnp.int32, (GROUP_SIZE, tm), 0)
    group_scores = []
    for g in range(N_GROUPS):
        vals = biased[g * GROUP_SIZE:(g + 1) * GROUP_SIZE]
        m1, i1 = _first_index_of_max(vals, in_group, GROUP_SIZE)
        m2 = jnp.max(jnp.where(in_group == i1, minus_inf, vals), axis=0, keepdims=True)
        group_scores.append(m1 + m2)
    gs = jnp.concatenate(group_scores, axis=0)

    gidx = lax.broadcasted_iota(jnp.int32, (N_GROUPS, tm), 0)
    group_sel = jnp.zeros((N_GROUPS, tm), jnp.bool_)
    for _ in range(TOPK_GROUPS):
        _, first = _first_index_of_max(gs, gidx, N_GROUPS)
        pick = gidx == first
        group_sel = jnp.logical_or(group_sel, pick)
        gs = jnp.where(pick, minus_inf, gs)

    eidx = lax.broadcasted_iota(jnp.int32, (N_EXPERTS, tm), 0)
    expert_group_sel = jnp.concatenate(
        [jnp.broadcast_to(group_sel[g:g + 1], (GROUP_SIZE, tm)) for g in range(N_GROUPS)], axis=0)
    cand = jnp.where(expert_group_sel, biased, NEG_INF)
    picks = []
    for _ in range(TOP_K):
        _, first = _first_index_of_max(cand, eidx, N_EXPERTS)
        pick = eidx == first
        picks.append(pick)
        cand = jnp.where(pick, minus_inf, cand)

    chosen = jnp.zeros((N_EXPERTS, tm), F32)
    for pick in picks:
        chosen = jnp.where(pick, 1.0, chosen)
    rank = _dot(chosen.astype(BF16), before_ref[...])
    count = jnp.sum(chosen, axis=1, keepdims=True).astype(jnp.int32)
    padded = (count + (SEG_ALIGN - 1)) & (-SEG_ALIGN)
    seg_start = _dot(lower_ref[...], jnp.broadcast_to(padded, (N_EXPERTS, LANES)).astype(BF16))[:, 0:1]
    slot_of = seg_start + rank
    slots = [jnp.sum(jnp.where(pick, slot_of, 0.0), axis=0, keepdims=True) for pick in picks]
    ws = [jnp.sum(jnp.where(pick, scores, 0.0), axis=0, keepdims=True) for pick in picks]
    norm = ROUTED_SCALE / (sum(ws) + 1e-20)
    slot_ref[...] = jnp.concatenate(slots, axis=0).astype(jnp.int32)
    gate_ref[...] = jnp.concatenate(ws, axis=0) * norm
    count_ref[0] = jnp.broadcast_to(count, (N_EXPERTS, LANES))


def _router(x1, g, sc, sh, wr_hi, wr_lo, b_router, S):
    T, D = x1.shape
    tm = MOE_TILE
    per_batch = S // tm
    E = N_EXPERTS
    tok = jnp.arange(tm, dtype=jnp.int32)
    before = (tok[:, None] < tok[None, :]).astype(BF16)
    ex = jnp.arange(E, dtype=jnp.int32)
    lower = (ex[None, :] < ex[:, None]).astype(BF16)
    return pl.pallas_call(
        _router_kernel,
        grid=(T // tm,),
        in_specs=[pl.BlockSpec((tm, D), lambda i: (i, 0)),
                  pl.BlockSpec((1, D), lambda i: (0, 0)),
                  pl.BlockSpec((1, 1, D), lambda i: (i // per_batch, 0, 0)),
                  pl.BlockSpec((1, 1, D), lambda i: (i // per_batch, 0, 0)),
                  pl.BlockSpec((E, D), lambda i: (0, 0)),
                  pl.BlockSpec((E, D), lambda i: (0, 0)),
                  pl.BlockSpec((E, 1), lambda i: (0, 0)),
                  pl.BlockSpec((tm, tm), lambda i: (0, 0)),
                  pl.BlockSpec((E, E), lambda i: (0, 0))],
        out_specs=[pl.BlockSpec((tm, D), lambda i: (i, 0)),
                   pl.BlockSpec((TOP_K, tm), lambda i: (0, i)),
                   pl.BlockSpec((TOP_K, tm), lambda i: (0, i)),
                   pl.BlockSpec((1, E, LANES), lambda i: (i, 0, 0))],
        out_shape=[jax.ShapeDtypeStruct((T, D), BF16),
                   jax.ShapeDtypeStruct((TOP_K, T), jnp.int32),
                   jax.ShapeDtypeStruct((TOP_K, T), F32),
                   jax.ShapeDtypeStruct((T // tm, E, LANES), jnp.int32)],
        compiler_params=_cparams(("arbitrary",)),
        name="router",
    )(x1, g, sc, sh, wr_hi, wr_lo, b_router, before, lower)


def _segment_pieces(tile, seg_start_ref, seg_dst_ref, seg_rows_ref, fn):
    def body(e, carry):
        idx = tile * N_EXPERTS + e
        local, dst, rows = seg_start_ref[idx], seg_dst_ref[idx], seg_rows_ref[idx]
        for size in SEG_PIECES:
            @pl.when(rows & size != 0)
            def _():
                done = rows & (-2 * size)
                fn(pl.multiple_of(local + done, SEG_ALIGN), pl.multiple_of(dst + done, SEG_ALIGN), size)
        return carry

    lax.fori_loop(0, N_EXPERTS, body, 0)


def _dispatch_kernel(seg_start_ref, seg_dst_ref, seg_rows_ref, h_ref, slot_ref, xs_in_ref, xs_ref, xloc, sem,
                     *, chunk):
    del xs_in_ref
    tile = pl.program_id(0)
    slots = slot_ref[...]
    h = h_ref[...]
    tm = h.shape[0]
    for r0 in range(0, xloc.shape[0], chunk):
        s = lax.broadcasted_iota(jnp.int32, (chunk, tm), 0) + r0
        onehot = jnp.zeros((chunk, tm), F32)
        for k in range(TOP_K):
            onehot = jnp.where(s == slots[k:k + 1, :], 1.0, onehot)
        xloc[r0:r0 + chunk, :] = _dot(onehot.astype(BF16), h).astype(BF16)

    def copy(local, dst, rows):
        return pltpu.make_async_copy(xloc.at[pl.ds(local, rows)], xs_ref.at[pl.ds(dst, rows)], sem)

    _segment_pieces(tile, seg_start_ref, seg_dst_ref, seg_rows_ref, lambda *a: copy(*a).start())
    _segment_pieces(tile, seg_start_ref, seg_dst_ref, seg_rows_ref, lambda *a: copy(*a).wait())


def _dispatch(seg_start, seg_dst, seg_rows, h2, slots, n_rows):
    T, D = h2.shape
    tm = MOE_TILE
    return pl.pallas_call(
        functools.partial(_dispatch_kernel, chunk=512),
        grid_spec=pltpu.PrefetchScalarGridSpec(
            num_scalar_prefetch=3,
            grid=(T // tm,),
            in_specs=[pl.BlockSpec((tm, D), lambda i, *_: (i, 0)),
                      pl.BlockSpec((TOP_K, tm), lambda i, *_: (0, i)),
                      pl.BlockSpec(memory_space=pl.ANY)],
            out_specs=pl.BlockSpec(memory_space=pl.ANY),
            scratch_shapes=[pltpu.VMEM((LOCAL_SLOTS, D), BF16), pltpu.SemaphoreType.DMA]),
        out_shape=jax.ShapeDtypeStruct((n_rows, D), BF16),
        input_output_aliases={5: 0},
        compiler_params=_cparams(("arbitrary",)),
        name="moe_dispatch",
    )(seg_start, seg_dst, seg_rows, h2, slots, jnp.zeros((n_rows, D), BF16))


def _expert_kernel(blk_expert_ref, blk_rows_ref, x_ref, w1_ref, w3_ref, w2_ref, y_ref):
    del blk_expert_ref
    rows = blk_rows_ref[pl.program_id(0)]

    @pl.when(rows > 0)
    def _():
        x = x_ref[...]
        a = _dot(x, w1_ref[0])
        hid = a * jax.nn.sigmoid(a) * _dot(x, w3_ref[0])
        y = _dot(hid.astype(BF16), w2_ref[0])
        r = lax.broadcasted_iota(jnp.int32, y.shape, 0)
        y_ref[...] = jnp.where(r < rows, y, 0.0).astype(BF16)

    @pl.when(rows == 0)
    def _():
        y_ref[...] = jnp.zeros_like(y_ref)


def _experts(blk_expert, blk_rows, xs, w1, w3, w2):
    n_rows, D = xs.shape
    F = EXPERT_FF
    R = EXPERT_BLOCK_ROWS
    return pl.pallas_call(
        _expert_kernel,
        grid_spec=pltpu.PrefetchScalarGridSpec(
            num_scalar_prefetch=2,
            grid=(n_rows // R,),
            in_specs=[pl.BlockSpec((R, D), lambda b, be, br: (b, 0)),
                      pl.BlockSpec((1, D, F), lambda b, be, br: (be[b], 0, 0)),
                      pl.BlockSpec((1, D, F), lambda b, be, br: (be[b], 0, 0)),
                      pl.BlockSpec((1, F, D), lambda b, be, br: (be[b], 0, 0))],
            out_specs=pl.BlockSpec((R, D), lambda b, be, br: (b, 0))),
        out_shape=jax.ShapeDtypeStruct((n_rows, D), BF16),
        compiler_params=_cparams(("arbitrary",)),
        name="moe_experts",
    )(blk_expert, blk_rows, xs, w1, w3, w2)


def _combine_kernel(seg_start_ref, seg_dst_ref, seg_rows_ref, ys_ref, slot_ref, gate_ref, h_ref, x_ref, gf_ref,
                    ws1_ref, ws3_ref, ws2_ref, o_ref, yloc, sem, *, chunk):
    tile = pl.program_id(0)
    yloc[...] = jnp.zeros_like(yloc)

    def copy(local, dst, rows):
        return pltpu.make_async_copy(ys_ref.at[pl.ds(dst, rows)], yloc.at[pl.ds(local, rows)], sem)

    _segment_pieces(tile, seg_start_ref, seg_dst_ref, seg_rows_ref, lambda *a: copy(*a).start())
    h = h_ref[...]
    a = _dot(h, ws1_ref[...])
    total = _dot((a * jax.nn.sigmoid(a) * _dot(h, ws3_ref[...])).astype(BF16), ws2_ref[...])
    _segment_pieces(tile, seg_start_ref, seg_dst_ref, seg_rows_ref, lambda *a: copy(*a).wait())

    slots = slot_ref[...]
    gates = gate_ref[...]
    tm = h.shape[0]
    for c0 in range(0, yloc.shape[0], chunk):
        s = lax.broadcasted_iota(jnp.int32, (tm, chunk), 1) + c0
        weights = jnp.zeros((tm, chunk), F32)
        for k in range(TOP_K):
            weights = jnp.where(s == slots[:, k:k + 1], gates[:, k:k + 1], weights)
        total = total + _dot(weights.astype(BF16), yloc[c0:c0 + chunk, :])
    o_ref[...] = x_ref[...] + gf_ref[0] * total


def _combine(seg_start, seg_dst, seg_rows, ys, slots_tok, gates_tok, h2, x1, g_f, ws1, ws3, ws2, S):
    T, D = x1.shape
    tm = MOE_TILE
    per_batch = S // tm
    F = ws1.shape[1]
    return pl.pallas_call(
        functools.partial(_combine_kernel, chunk=512),
        grid_spec=pltpu.PrefetchScalarGridSpec(
            num_scalar_prefetch=3,
            grid=(T // tm,),
            in_specs=[pl.BlockSpec(memory_space=pl.ANY),
                      pl.BlockSpec((tm, TOP_K), lambda i, *_: (i, 0)),
                      pl.BlockSpec((tm, TOP_K), lambda i, *_: (i, 0)),
                      pl.BlockSpec((tm, D), lambda i, *_: (i, 0)),
                      pl.BlockSpec((tm, D), lambda i, *_: (i, 0)),
                      pl.BlockSpec((1, 1, D), lambda i, *_: (i // per_batch, 0, 0)),
                      pl.BlockSpec((D, F), lambda i, *_: (0, 0)),
                      pl.BlockSpec((D, F), lambda i, *_: (0, 0)),
                      pl.BlockSpec((F, D), lambda i, *_: (0, 0))],
            out_specs=pl.BlockSpec((tm, D), lambda i, *_: (i, 0)),
            scratch_shapes=[pltpu.VMEM((LOCAL_SLOTS, D), BF16), pltpu.SemaphoreType.DMA]),
        out_shape=jax.ShapeDtypeStruct((T, D), F32),
        compiler_params=_cparams(("arbitrary",)),
        name="moe_combine",
    )(seg_start, seg_dst, seg_rows, ys, slots_tok, gates_tok, h2, x1, g_f, ws1, ws3, ws2)


def _moe(x1, norm_g, sc_f, sh_f, g_f, w_router, b_router, w1, w3, w2, ws1, ws3, ws2, S):
    T, D = x1.shape
    E = N_EXPERTS
    R = EXPERT_BLOCK_ROWS
    n_tiles = T // MOE_TILE
    wr_t = w_router.T
    wr_hi = wr_t.astype(BF16)
    wr_lo = (wr_t - wr_hi.astype(F32)).astype(BF16)
    h2, slots, gates, counts = _router(x1, norm_g.reshape(1, D), sc_f, sh_f, wr_hi, wr_lo, b_router.reshape(E, 1), S)

    seg_rows = (counts[:, :, 0] + (SEG_ALIGN - 1)) // SEG_ALIGN * SEG_ALIGN
    seg_start = jnp.cumsum(seg_rows, axis=1) - seg_rows
    expert_rows = jnp.sum(seg_rows, axis=0)
    region = (expert_rows + (R - 1)) // R * R
    region_end = jnp.cumsum(region)
    region_start = region_end - region
    seg_dst = region_start[None, :] + jnp.cumsum(seg_rows, axis=0) - seg_rows
    n_blocks = -(-(T * TOP_K + n_tiles * E * (SEG_ALIGN - 1) + E * (R - SEG_ALIGN)) // R)
    blk_first = jnp.arange(n_blocks, dtype=jnp.int32) * R
    blk_expert = jnp.minimum(jnp.searchsorted(region_end, blk_first, side='right'), E - 1).astype(jnp.int32)
    blk_rows = jnp.clip(expert_rows[blk_expert] - (blk_first - region_start[blk_expert]), 0, R).astype(jnp.int32)

    flat = lambda a: a.reshape(-1).astype(jnp.int32)
    tables = (flat(seg_start), flat(seg_dst), flat(seg_rows))
    xs = _dispatch(*tables, h2, slots, n_blocks * R)
    ys = _experts(blk_expert, blk_rows, xs, w1.astype(BF16), w3.astype(BF16), w2.astype(BF16))
    return _combine(*tables, ys, slots.T, gates.T, h2, x1, g_f,
                    ws1.astype(BF16), ws3.astype(BF16), ws2.astype(BF16), S)


def kernel(x, c, positions, w_mod, b_mod, norm_mix_g, norm_ffn_g, w_in, qn_g, kn_g, lam_q1, lam_k1, lam_q2, lam_k2, subln_g, w_proj_a, w_dw, b_dw, conv_ln_g, conv_ln_b, w_proj_b, b_proj_b, w_proj_c, w_out, w_router, b_router, w1, w3, w2, ws1, ws3, ws2):
    B, S, D = x.shape
    T = B * S
    depth = w_mod.shape[0]
    xt = x.reshape(T, D)

    inv = ROPE_THETA ** (-jnp.arange(0, DA_HEAD_DIM, 2, dtype=F32) / DA_HEAD_DIM)
    inv_lanes = jnp.tile(inv, LANES // (DA_HEAD_DIM // 2)).reshape(1, LANES)
    pos_lanes = jnp.broadcast_to(positions.astype(F32).reshape(T, 1), (T, LANES))
    cos, sin = _rope_tables(pos_lanes, inv_lanes)

    mod = _modulation(c, w_mod, b_mod)

    seg = jnp.arange(DA_WIDTH, dtype=jnp.int32) // DA_HEAD_DIM
    bd = jnp.where(seg[:, None] == seg[None, :], 1.0 / DA_HEAD_DIM, 0.0).astype(BF16)
    kk = jnp.arange(SB_BLOCK, dtype=jnp.int32)
    u = jnp.where(kk[:, None] >= kk[None, :], -1.0, 0.0).astype(BF16)
    u = jnp.concatenate([u, u], axis=0)
    n_seg = DA_WIDTH // DA_HEAD_DIM

    for l in range(depth):
        lambda_init = 0.8 - 0.6 * math.exp(-0.3 * l)
        sh_m, sc_m, g_m, sh_f, sc_f, g_f = [m.reshape(B, 1, D) for m in jnp.split(mod[l], 6, axis=-1)]

        proj = _norm_proj(xt, norm_mix_g[l].reshape(1, D), sc_m, sh_m, w_in[l].astype(BF16), S)

        q_r, k_r = _qk_prep(proj, cos, sin, bd,
                            jnp.tile(qn_g[l], n_seg).reshape(1, DA_WIDTH),
                            jnp.tile(kn_g[l], n_seg).reshape(1, DA_WIDTH))
        lam = (jnp.exp(jnp.sum(lam_q1[l] * lam_k1[l])) - jnp.exp(jnp.sum(lam_q2[l] * lam_k2[l])) + lambda_init)
        score_bound = (DA_HEAD_DIM ** 0.5) * jnp.max(jnp.abs(qn_g[l])) * jnp.max(jnp.abs(kn_g[l]))
        attn = functools.partial(_diff_attn, B=B, S=S, out_scale=1.0 - lambda_init)
        o_a = lax.cond(score_bound <= MAX_FIXED_SHIFT,
                       functools.partial(attn, online=False), functools.partial(attn, online=True),
                       score_bound.reshape(1, 1).astype(F32), jnp.full((1, LANES), lam, F32), q_r, k_r, proj,
                       subln_g[l].reshape(1, LANES))
        cb = _conv_module(proj, w_dw[l], b_dw[l].reshape(1, CONV_CH), conv_ln_g[l].reshape(1, CONV_CH),
                          conv_ln_b[l].reshape(1, CONV_CH), S)
        o_c = _sb_attn(proj, u, B, S)
        x1 = _merge(o_a, cb, o_c, proj, xt, g_m, w_proj_a[l].astype(BF16), w_proj_b[l].astype(BF16),
                    b_proj_b[l].reshape(1, D), w_proj_c[l].astype(BF16), w_out[l].astype(BF16), S)

        xt = _moe(x1, norm_ffn_g[l], sc_f, sh_f, g_f, w_router[l], b_router[l],
                  w1[l], w3[l], w2[l], ws1[l], ws3[l], ws2[l], S)

    return xt.reshape(B, S, D)
```

```python
import functools
import math

import jax
import jax.numpy as jnp
from jax import lax
from jax.experimental import pallas as pl
from jax.experimental.pallas import tpu as pltpu

F32 = jnp.float32
BF16 = jnp.bfloat16

D_MODEL = 1024
DA_HEADS = 4
DA_HEAD_DIM = 64
DA_WIDTH = 512
CONV_CH = 512
CONV_WIDTH = 31
SB_WIDTH = 512
SB_HEAD_DIM = 64
IN_COLS = 7168
ROPE_THETA = 10000.0
EPS = 1e-6
NEG_INF = -1e30
LOG2E = math.log2(math.e)
MAX_FIXED_SHIFT = 40.0
SB_BLOCK = 256
SB_BLOCKS_PER_TRIP = 4
N_EXPERTS = 64
TOP_K = 8
N_GROUPS = 8
TOPK_GROUPS = 4
GROUP_SIZE = N_EXPERTS // N_GROUPS
EXPERT_FF = 256
ROUTED_SCALE = 2.5

MOE_TILE = 512
SEG_ALIGN = 16
SEG_PIECES = tuple(SEG_ALIGN << j for j in reversed(range((MOE_TILE // SEG_ALIGN).bit_length())))
LOCAL_SLOTS = -(-(MOE_TILE * TOP_K + N_EXPERTS * (SEG_ALIGN - 1)) // 512) * 512
SEG_RARE = 128
EXPERT_BLOCK_ROWS = 512
LANES = 128
SUBLANES = 8
CONV_HALO = 32

COL_QA, COL_KA, COL_VA, COL_UB, COL_QC, COL_KC, COL_VC, COL_GATES = 0, 512, 1024, 1536, 2560, 3072, 3584, 4096

VMEM_LIMIT = 48 * 1024 * 1024


def _cparams(sem):
    return pltpu.CompilerParams(dimension_semantics=sem, vmem_limit_bytes=VMEM_LIMIT)


def _dot(a, b):
    return jnp.dot(a, b, preferred_element_type=F32)


def _dot_nt(a, b):
    return lax.dot_general(a, b, (((1,), (1,)), ((), ())), preferred_element_type=F32)


def _split_hi_lo(x):
    hi = x.astype(BF16)
    lo = (x - hi.astype(F32)).astype(BF16)
    return hi, lo


def _mod_kernel(c_ref, w_ref, b_ref, o_ref):
    c = c_ref[...]
    c_act = c * jax.nn.sigmoid(c)
    o_ref[0] = jnp.dot(c_act, w_ref[0], precision=lax.Precision.HIGHEST,
                       preferred_element_type=F32) + b_ref[0]


def _modulation(c, w_mod, b_mod):
    L, D, N = w_mod.shape
    B = c.shape[0]
    tn = 1536
    return pl.pallas_call(
        _mod_kernel,
        grid=(L, N // tn),
        in_specs=[pl.BlockSpec((B, D), lambda l, j: (0, 0)),
                  pl.BlockSpec((1, D, tn), lambda l, j: (l, 0, j)),
                  pl.BlockSpec((1, 1, tn), lambda l, j: (l, 0, j))],
        out_specs=pl.BlockSpec((1, B, tn), lambda l, j: (l, 0, j)),
        out_shape=jax.ShapeDtypeStruct((L, B, N), F32),
        compiler_params=_cparams(("arbitrary", "arbitrary")),
        name="modulation",
    )(c, w_mod, b_mod.reshape(L, 1, N))


def _rope_kernel(pos_ref, inv_ref, cos_ref, sin_ref):
    ang = pos_ref[...] * inv_ref[...]
    lane = lax.broadcasted_iota(jnp.int32, ang.shape, 1)
    first_half = (lane % DA_HEAD_DIM) < (DA_HEAD_DIM // 2)
    cos_ref[...] = jnp.cos(ang)
    s = jnp.sin(ang)
    sin_ref[...] = jnp.where(first_half, -s, s)


def _rope_tables(pos_lanes, inv_lanes):
    T = pos_lanes.shape[0]
    tm = 2048
    return pl.pallas_call(
        _rope_kernel,
        grid=(T // tm,),
        in_specs=[pl.BlockSpec((tm, LANES), lambda i: (i, 0)),
                  pl.BlockSpec((1, LANES), lambda i: (0, 0))],
        out_specs=[pl.BlockSpec((tm, LANES), lambda i: (i, 0))] * 2,
        out_shape=[jax.ShapeDtypeStruct((T, LANES), F32)] * 2,
        compiler_params=_cparams(("arbitrary",)),
        name="rope_tables",
    )(pos_lanes, inv_lanes)


def _ada_rms(x, g, sc, sh):
    ms = jnp.mean(x * x, axis=-1, keepdims=True)
    return x * lax.rsqrt(ms + EPS) * g * (1.0 + sc) + sh


def _norm_proj_kernel(x_ref, g_ref, sc_ref, sh_ref, w_ref, o_ref, h_scr):
    @pl.when(pl.program_id(1) == 0)
    def _():
        h_scr[...] = _ada_rms(x_ref[...], g_ref[...], sc_ref[0], sh_ref[0]).astype(BF16)

    o_ref[...] = _dot(h_scr[...], w_ref[...]).astype(BF16)


def _norm_proj(x, g, sc, sh, w_bf16, S):
    T, D = x.shape
    N = w_bf16.shape[1]
    tm, tn = 1024, 1024
    per_batch = S // tm
    return pl.pallas_call(
        _norm_proj_kernel,
        grid=(T // tm, N // tn),
        in_specs=[pl.BlockSpec((tm, D), lambda i, j: (i, 0)),
                  pl.BlockSpec((1, D), lambda i, j: (0, 0)),
                  pl.BlockSpec((1, 1, D), lambda i, j: (i // per_batch, 0, 0)),
                  pl.BlockSpec((1, 1, D), lambda i, j: (i // per_batch, 0, 0)),
                  pl.BlockSpec((D, tn), lambda i, j: (0, j))],
        out_specs=pl.BlockSpec((tm, tn), lambda i, j: (i, j)),
        out_shape=jax.ShapeDtypeStruct((T, N), BF16),
        scratch_shapes=[pltpu.VMEM((tm, D), BF16)],
        compiler_params=_cparams(("arbitrary", "arbitrary")),
        name="norm_proj",
    )(x, g, sc, sh, w_bf16)


def _qk_prep_kernel(q_ref, k_ref, cos_ref, sin_ref, bd_ref, qg_ref, kg_ref, qo_ref, ko_ref):
    cos = cos_ref[...]
    sin = sin_ref[...]
    bd = bd_ref[...]
    lane = lax.broadcasted_iota(jnp.int32, cos.shape, 1)
    first_half = (lane % DA_HEAD_DIM) < (DA_HEAD_DIM // 2)
    half = DA_HEAD_DIM // 2

    def prep(x_ref, g_ref, o_ref, scale):
        x = x_ref[...].astype(F32)
        hi, lo = _split_hi_lo(x * x)
        ms = _dot(hi, bd) + _dot(lo, bd)
        y = x * lax.rsqrt(ms + EPS) * g_ref[...]
        for c in range(DA_WIDTH // LANES):
            yc = y[:, c * LANES:(c + 1) * LANES]
            swapped = jnp.where(first_half, pltpu.roll(yc, LANES - half, 1), pltpu.roll(yc, half, 1))
            o_ref[:, c * LANES:(c + 1) * LANES] = ((yc * cos + swapped * sin) * scale).astype(BF16)

    prep(q_ref, qg_ref, qo_ref, DA_HEAD_DIM ** -0.5)
    prep(k_ref, kg_ref, ko_ref, 1.0)


def _qk_prep(proj, cos, sin, bd, qg, kg):
    T = proj.shape[0]
    tm = 1024
    W = DA_WIDTH
    return pl.pallas_call(
        _qk_prep_kernel,
        grid=(T // tm,),
        in_specs=[pl.BlockSpec((tm, W), lambda i: (i, COL_QA // W)),
                  pl.BlockSpec((tm, W), lambda i: (i, COL_KA // W)),
                  pl.BlockSpec((tm, LANES), lambda i: (i, 0)),
                  pl.BlockSpec((tm, LANES), lambda i: (i, 0)),
                  pl.BlockSpec((W, W), lambda i: (0, 0)),
                  pl.BlockSpec((1, W), lambda i: (0, 0)),
                  pl.BlockSpec((1, W), lambda i: (0, 0))],
        out_specs=[pl.BlockSpec((tm, W), lambda i: (i, 0))] * 2,
        out_shape=[jax.ShapeDtypeStruct((T, W), BF16)] * 2,
        compiler_params=_cparams(("arbitrary",)),
        name="qk_prep",
    )(proj, proj, cos, sin, bd, qg, kg)


def _diff_attn_kernel(shift_ref, lam_ref, q_ref, k_ref, v_ref, g_ref, o_ref, acc1, acc2, *, tq, out_scale, online):
    qi = pl.program_id(2)
    qf = q_ref[...].astype(F32)
    lane = lax.broadcasted_iota(jnp.int32, qf.shape, 1)
    qs = (jnp.where(lane < DA_HEAD_DIM, qf, 0.0).astype(BF16),
          jnp.where(lane >= DA_HEAD_DIM, qf, 0.0).astype(BF16))
    accs = (acc1, acc2)
    acc1[...] = jnp.zeros_like(acc1)
    acc2[...] = jnp.zeros_like(acc2)
    row = lax.broadcasted_iota(jnp.int32, (tq, tq), 0)
    col = lax.broadcasted_iota(jnp.int32, (tq, tq), 1)
    causal = col <= row
    shift = shift_ref[...]

    def step(kj, carry, masked):
        start = pl.multiple_of(kj * tq, tq)
        k = k_ref[pl.ds(start, tq), :]
        v = v_ref[pl.ds(start, tq), :]
        ss = [_dot_nt(q, k) for q in qs]
        if masked:
            ss = [jnp.where(causal, s, NEG_INF) for s in ss]
        if online:
            ms, ls = carry
            m_new = [jnp.maximum(m, jnp.max(s, axis=1, keepdims=True)) for m, s in zip(ms, ss)]
            ps = [jnp.exp(s - m) for s, m in zip(ss, m_new)]
            alphas = [jnp.exp(m - mn) for m, mn in zip(ms, m_new)]
            for acc, alpha, p in zip(accs, alphas, ps):
                acc[...] = alpha * acc[...] + _dot(p.astype(BF16), v)
            ls = [alpha * l + jnp.sum(p, axis=1, keepdims=True) for alpha, l, p in zip(alphas, ls, ps)]
            return tuple(m_new), tuple(ls)
        ps = [jnp.exp(s - shift) for s in ss]
        for acc, p in zip(accs, ps):
            acc[...] += _dot(p.astype(BF16), v)
        return tuple(l + jnp.sum(p, axis=1, keepdims=True) for l, p in zip(carry, ps))

    zero = jnp.zeros((tq, 1), F32)
    if online:
        neg = jnp.full((tq, 1), NEG_INF, F32)
        init = ((neg, neg), (zero, zero))
    else:
        init = (zero, zero)
    carry = lax.fori_loop(0, qi, lambda kj, c: step(kj, c, False), init)
    carry = step(qi, carry, True)
    l1, l2 = carry[1] if online else carry
    o = acc1[...] / l1 - lam_ref[...] * (acc2[...] / l2)
    ms = jnp.mean(o * o, axis=-1, keepdims=True)
    o_ref[...] = (o * lax.rsqrt(ms + EPS) * g_ref[...] * out_scale).astype(BF16)


def _diff_attn(shift, lam_lanes, q_r, k_r, proj, subln_g, B, S, out_scale, online):
    T = q_r.shape[0]
    tq = 512
    nq = S // tq
    return pl.pallas_call(
        functools.partial(_diff_attn_kernel, tq=tq, out_scale=out_scale, online=online),
        grid=(B, DA_HEADS, nq),
        in_specs=[pl.BlockSpec((1, 1), lambda b, h, i: (0, 0)),
                  pl.BlockSpec((1, LANES), lambda b, h, i: (0, 0)),
                  pl.BlockSpec((tq, LANES), lambda b, h, i: (b * nq + i, h)),
                  pl.BlockSpec((S, LANES), lambda b, h, i: (b, h)),
                  pl.BlockSpec((S, LANES), lambda b, h, i: (b, COL_VA // LANES + h)),
                  pl.BlockSpec((1, LANES), lambda b, h, i: (0, 0))],
        out_specs=pl.BlockSpec((tq, LANES), lambda b, h, i: (b * nq + i, h)),
        out_shape=jax.ShapeDtypeStruct((T, DA_WIDTH), BF16),
        scratch_shapes=[pltpu.VMEM((tq, LANES), F32), pltpu.VMEM((tq, LANES), F32)],
        compiler_params=_cparams(("arbitrary", "arbitrary", "arbitrary")),
        name="diff_attn_online" if online else "diff_attn",
    )(shift, lam_lanes, q_r, k_r, proj, subln_g)


def _sb_attn_kernel(q_ref, k_ref, v_ref, u_ref, o_ref, acc, *, tq):
    qi = pl.program_id(2)
    qf = q_ref[...].astype(F32) * (SB_HEAD_DIM ** -0.5 * LOG2E)
    lane = lax.broadcasted_iota(jnp.int32, qf.shape, 1)
    qs = (jnp.where(lane < SB_HEAD_DIM, qf, 0.0).astype(BF16),
          jnp.where(lane >= SB_HEAD_DIM, qf, 0.0).astype(BF16))
    uu = u_ref[...]
    acc[...] = jnp.zeros_like(acc)
    row = lax.broadcasted_iota(jnp.int32, (tq, tq), 0)
    col = lax.broadcasted_iota(jnp.int32, (tq, tq), 1)
    before = col < row

    def step(kjs, rs, masked):
        ks = [k_ref[pl.ds(pl.multiple_of(kj * tq, tq), tq), :] for kj in kjs]
        vs = [v_ref[pl.ds(pl.multiple_of(kj * tq, tq), tq), :] for kj in kjs]
        chains = [(h, b) for b in range(len(kjs)) for h in range(2)]
        zs = [_dot_nt(qs[h], ks[b]) for h, b in chains]
        ps, hls = [], []
        for z in zs:
            neg_abs = pltpu.bitcast(pltpu.bitcast(z, jnp.uint32) | jnp.uint32(0x80000000), F32)
            p = jnp.maximum(z, 0.0) + jnp.log2(1.0 + jnp.exp2(neg_abs))
            if masked:
                p = jnp.where(before, p, 0.0)
            hi, lo = _split_hi_lo(p)
            ps.append(jnp.sum(p, axis=1, keepdims=True))
            hls.append(jnp.concatenate([hi, lo], axis=1))
        mms = [_dot(hl, uu) for hl in hls]
        rs = list(rs)
        for c, (h, b) in enumerate(chains):
            a = jnp.exp2(zs[c] + mms[c] + rs[h])
            if masked:
                a = jnp.where(before, a, 0.0)
            acc[h * tq:(h + 1) * tq, :] += _dot(a.astype(BF16), vs[b])
            rs[h] = rs[h] - ps[c]
        return tuple(rs)

    zero = jnp.zeros((tq, 1), F32)
    rs = step([qi], (zero, zero), True)
    def run(first, count, c):
        return step([first - d for d in range(count)], c, False)

    nb = SB_BLOCKS_PER_TRIP
    rs = lax.fori_loop(0, qi // nb, lambda t, c: run(qi - 1 - nb * t, nb, c), rs)
    left = qi % nb
    part = nb // 2
    while part >= 1:
        rs = lax.cond(left & part != 0, functools.partial(run, (left & (2 * part - 1)) - 1, part),
                      lambda c: c, rs)
        part //= 2

    o_ref[...] = jnp.where(lane < SB_HEAD_DIM, acc[0:tq, :], acc[tq:2 * tq, :]).astype(BF16)


def _sb_attn(proj, u, B, S):
    T = proj.shape[0]
    tq = SB_BLOCK
    nq = S // tq
    return pl.pallas_call(
        functools.partial(_sb_attn_kernel, tq=tq),
        grid=(B, SB_WIDTH // LANES, nq),
        in_specs=[pl.BlockSpec((tq, LANES), lambda b, p, i: (b * nq + i, COL_QC // LANES + p)),
                  pl.BlockSpec((S, LANES), lambda b, p, i: (b, COL_KC // LANES + p)),
                  pl.BlockSpec((S, LANES), lambda b, p, i: (b, COL_VC // LANES + p)),
                  pl.BlockSpec((2 * tq, tq), lambda b, p, i: (0, 0))],
        out_specs=pl.BlockSpec((tq, LANES), lambda b, p, i: (b * nq + i, p)),
        out_shape=jax.ShapeDtypeStruct((T, SB_WIDTH), BF16),
        scratch_shapes=[pltpu.VMEM((2 * tq, LANES), F32)],
        compiler_params=_cparams(("arbitrary", "arbitrary", "arbitrary")),
        name="sb_attn",
    )(proj, proj, proj, u)


def _conv_kernel(a_ref, g_ref, ah_ref, gh_ref, w_ref, b_ref, lg_ref, lb_ref, o_ref, hbuf, *, tm, per_batch, rows):
    i = pl.program_id(0)
    halo = ah_ref[...].astype(F32) * jax.nn.sigmoid(gh_ref[...].astype(F32))
    hbuf[0:CONV_HALO, :] = jnp.where(i % per_batch == 0, 0.0, halo)
    hbuf[CONV_HALO:, :] = a_ref[...].astype(F32) * jax.nn.sigmoid(g_ref[...].astype(F32))
    w = w_ref[...]
    for r0 in range(0, tm, rows):
        acc = jnp.broadcast_to(b_ref[...], (rows, CONV_CH))
        for b in range(SUBLANES):
            part = None
            for a in range((CONV_WIDTH - 1 - b) // SUBLANES + 1):
                lo = r0 + CONV_HALO - SUBLANES - SUBLANES * a
                tap = CONV_WIDTH - 1 - SUBLANES * a - b
                term = hbuf[lo:lo + rows + SUBLANES, :] * w[tap:tap + 1, :]
                part = term if part is None else part + term
            acc = acc + part[SUBLANES - b:SUBLANES - b + rows, :]
        mu = jnp.mean(acc, axis=-1, keepdims=True)
        d = acc - mu
        var = jnp.mean(d * d, axis=-1, keepdims=True)
        y = d * lax.rsqrt(var + EPS) * lg_ref[...] + lb_ref[...]
        o_ref[r0:r0 + rows, :] = (y * jax.nn.sigmoid(y)).astype(BF16)


def _conv_module(proj, w_dw, b_dw, ln_g, ln_b, S):
    T = proj.shape[0]
    tm = 512
    per_batch = S // tm
    C = CONV_CH
    ca, cg = COL_UB // C, COL_UB // C + 1
    halo_blocks = tm // CONV_HALO

    def halo_idx(col):
        return lambda i: (jnp.maximum(i * halo_blocks - 1, 0), col)

    return pl.pallas_call(
        functools.partial(_conv_kernel, tm=tm, per_batch=per_batch, rows=128),
        grid=(T // tm,),
        in_specs=[pl.BlockSpec((tm, C), lambda i: (i, ca)),
                  pl.BlockSpec((tm, C), lambda i: (i, cg)),
                  pl.BlockSpec((CONV_HALO, C), halo_idx(ca)),
                  pl.BlockSpec((CONV_HALO, C), halo_idx(cg)),
                  pl.BlockSpec((CONV_WIDTH, C), lambda i: (0, 0)),
                  pl.BlockSpec((1, C), lambda i: (0, 0)),
                  pl.BlockSpec((1, C), lambda i: (0, 0)),
                  pl.BlockSpec((1, C), lambda i: (0, 0))],
        out_specs=pl.BlockSpec((tm, C), lambda i: (i, 0)),
        out_shape=jax.ShapeDtypeStruct((T, C), BF16),
        scratch_shapes=[pltpu.VMEM((tm + CONV_HALO, C), F32)],
        compiler_params=_cparams(("arbitrary",)),
        name="conv_module",
    )(proj, proj, proj, proj, w_dw, b_dw, ln_g, ln_b)


def _merge_kernel(oa_ref, cb_ref, oc_ref, g0_ref, g1_ref, g2_ref, x_ref, gm_ref,
                  wa_ref, wb_ref, bb_ref, wc_ref, wo_ref, o_ref):
    y_a = _dot(oa_ref[...], wa_ref[...])
    y_b = _dot(cb_ref[...], wb_ref[...]) + bb_ref[...]
    y_c = _dot(oc_ref[...], wc_ref[...])
    merged = (jax.nn.sigmoid(g0_ref[...].astype(F32)) * y_a
              + jax.nn.sigmoid(g1_ref[...].astype(F32)) * y_b
              + jax.nn.sigmoid(g2_ref[...].astype(F32)) * y_c)
    o_ref[...] = x_ref[...] + gm_ref[0] * _dot(merged.astype(BF16), wo_ref[...])


def _merge(o_a, cb, o_c, proj, x, g_m, wa, wb, bb, wc, wo, S):
    T, D = x.shape
    tm = 512
    per_batch = S // tm
    W = DA_WIDTH
    gcol = COL_GATES // D
    branch = pl.BlockSpec((tm, W), lambda i: (i, 0))
    wspec = pl.BlockSpec((W, D), lambda i: (0, 0))
    return pl.pallas_call(
        _merge_kernel,
        grid=(T // tm,),
        in_specs=[branch, branch, branch,
                  pl.BlockSpec((tm, D), lambda i: (i, gcol)),
                  pl.BlockSpec((tm, D), lambda i: (i, gcol + 1)),
                  pl.BlockSpec((tm, D), lambda i: (i, gcol + 2)),
                  pl.BlockSpec((tm, D), lambda i: (i, 0)),
                  pl.BlockSpec((1, 1, D), lambda i: (i // per_batch, 0, 0)),
                  wspec, wspec, pl.BlockSpec((1, D), lambda i: (0, 0)), wspec,
                  pl.BlockSpec((D, D), lambda i: (0, 0))],
        out_specs=pl.BlockSpec((tm, D), lambda i: (i, 0)),
        out_shape=jax.ShapeDtypeStruct((T, D), F32),
        compiler_params=_cparams(("arbitrary",)),
        name="merge",
    )(o_a, cb, o_c, proj, proj, proj, x, g_m, wa, wb, bb, wc, wo)


def _first_index_of_max(vals, idx, size):
    mx = jnp.max(vals, axis=0, keepdims=True)
    first = jnp.min(jnp.where(vals == mx, idx, size), axis=0, keepdims=True)
    return mx, first


def _router_kernel(x_ref, g_ref, sc_ref, sh_ref, wh_ref, wl_ref, b_ref, before_ref, lower_ref,
                   h_ref, slot_ref, gate_ref, count_ref):
    h = _ada_rms(x_ref[...], g_ref[...], sc_ref[0], sh_ref[0])
    h_ref[...] = h.astype(BF16)
    h_hi, h_lo = _split_hi_lo(h)
    logits = _dot_nt(wh_ref[...], h_hi) + _dot_nt(wh_ref[...], h_lo) + _dot_nt(wl_ref[...], h_hi)
    scores = jax.nn.sigmoid(logits)
    biased = scores + b_ref[...]
    tm = scores.shape[1]
    minus_inf = -jnp.inf

    in_group = lax.broadcasted_iota(jnp.int32, (GROUP_SIZE, tm), 0)
    group_scores = []
    for g in range(N_GROUPS):
        vals = biased[g * GROUP_SIZE:(g + 1) * GROUP_SIZE]
        m1, i1 = _first_index_of_max(vals, in_group, GROUP_SIZE)
        m2 = jnp.max(jnp.where(in_group == i1, minus_inf, vals), axis=0, keepdims=True)
        group_scores.append(m1 + m2)
    gs = jnp.concatenate(group_scores, axis=0)

    gidx = lax.broadcasted_iota(jnp.int32, (N_GROUPS, tm), 0)
    group_sel = jnp.zeros((N_GROUPS, tm), jnp.bool_)
    for _ in range(TOPK_GROUPS):
        _, first = _first_index_of_max(gs, gidx, N_GROUPS)
        pick = gidx == first
        group_sel = jnp.logical_or(group_sel, pick)
        gs = jnp.where(pick, minus_inf, gs)

    eidx = lax.broadcasted_iota(jnp.int32, (N_EXPERTS, tm), 0)
    expert_group_sel = jnp.concatenate(
        [jnp.broadcast_to(group_sel[g:g + 1], (GROUP_SIZE, tm)) for g in range(N_GROUPS)], axis=0)
    cand = jnp.where(expert_group_sel, biased, NEG_INF)
    picks = []
    for _ in range(TOP_K):
        _, first = _first_index_of_max(cand, eidx, N_EXPERTS)
        pick = eidx == first
        picks.append(pick)
        cand = jnp.where(pick, minus_inf, cand)

    chosen = jnp.zeros((N_EXPERTS, tm), F32)
    for pick in picks:
        chosen = jnp.where(pick, 1.0, chosen)
    rank = _dot(chosen.astype(BF16), before_ref[...])
    count = jnp.sum(chosen, axis=1, keepdims=True).astype(jnp.int32)
    padded = (count + (SEG_ALIGN - 1)) & (-SEG_ALIGN)
    seg_start = _dot(lower_ref[...], jnp.broadcast_to(padded, (N_EXPERTS, LANES)).astype(BF16))[:, 0:1]
    slot_of = seg_start + rank
    slots = [jnp.sum(jnp.where(pick, slot_of, 0.0), axis=0, keepdims=True) for pick in picks]
    ws = [jnp.sum(jnp.where(pick, scores, 0.0), axis=0, keepdims=True) for pick in picks]
    norm = ROUTED_SCALE / (sum(ws) + 1e-20)
    slot_ref[...] = jnp.concatenate(slots, axis=0).astype(jnp.int32)
    gate_ref[...] = jnp.concatenate(ws, axis=0) * norm
    count_ref[0] = jnp.broadcast_to(count, (N_EXPERTS, LANES))


def _router(x1, g, sc, sh, wr_hi, wr_lo, b_router, S):
    T, D = x1.shape
    tm = MOE_TILE
    per_batch = S // tm
    E = N_EXPERTS
    tok = jnp.arange(tm, dtype=jnp.int32)
    before = (tok[:, None] < tok[None, :]).astype(BF16)
    ex = jnp.arange(E, dtype=jnp.int32)
    lower = (ex[None, :] < ex[:, None]).astype(BF16)
    return pl.pallas_call(
        _router_kernel,
        grid=(T // tm,),
        in_specs=[pl.BlockSpec((tm, D), lambda i: (i, 0)),
                  pl.BlockSpec((1, D), lambda i: (0, 0)),
                  pl.BlockSpec((1, 1, D), lambda i: (i // per_batch, 0, 0)),
                  pl.BlockSpec((1, 1, D), lambda i: (i // per_batch, 0, 0)),
                  pl.BlockSpec((E, D), lambda i: (0, 0)),
                  pl.BlockSpec((E, D), lambda i: (0, 0)),
                  pl.BlockSpec((E, 1), lambda i: (0, 0)),
                  pl.BlockSpec((tm, tm), lambda i: (0, 0)),
                  pl.BlockSpec((E, E), lambda i: (0, 0))],
        out_specs=[pl.BlockSpec((tm, D), lambda i: (i, 0)),
                   pl.BlockSpec((TOP_K, tm), lambda i: (0, i)),
                   pl.BlockSpec((TOP_K, tm), lambda i: (0, i)),
                   pl.BlockSpec((1, E, LANES), lambda i: (i, 0, 0))],
        out_shape=[jax.ShapeDtypeStruct((T, D), BF16),
                   jax.ShapeDtypeStruct((TOP_K, T), jnp.int32),
                   jax.ShapeDtypeStruct((TOP_K, T), F32),
                   jax.ShapeDtypeStruct((T // tm, E, LANES), jnp.int32)],
        compiler_params=_cparams(("arbitrary",)),
        name="router",
    )(x1, g, sc, sh, wr_hi, wr_lo, b_router, before, lower)


def _segment_pieces(tile, seg_start_ref, seg_dst_ref, seg_rows_ref, fn):
    def body(e, carry):
        idx = tile * N_EXPERTS + e
        local, dst, rows = seg_start_ref[idx], seg_dst_ref[idx], seg_rows_ref[idx]

        def pieces(sizes):
            for size in sizes:
                @pl.when(rows & size != 0)
                def _():
                    done = rows & (-2 * size)
                    fn(pl.multiple_of(local + done, SEG_ALIGN), pl.multiple_of(dst + done, SEG_ALIGN), size)

        pieces([s for s in SEG_PIECES if s < SEG_RARE])
        pl.when(rows >= SEG_RARE)(lambda: pieces([s for s in SEG_PIECES if s >= SEG_RARE]))
        return carry

    lax.fori_loop(0, N_EXPERTS, body, 0)


def _wait_pieces(tile, piece_count_ref, wait_one):
    for j, size in enumerate(SEG_PIECES):
        def body(_, carry, size=size):
            wait_one(size)
            return carry

        lax.fori_loop(0, piece_count_ref[tile * len(SEG_PIECES) + j], body, 0)


def _dispatch_kernel(seg_start_ref, seg_dst_ref, seg_rows_ref, piece_count_ref, h_ref, slot_ref, xs_in_ref, xs_ref,
                     xloc, sem, *, chunk):
    del xs_in_ref
    tile = pl.program_id(0)
    slots = slot_ref[...]
    h = h_ref[...]
    tm = h.shape[0]
    for r0 in range(0, xloc.shape[0], chunk):
        s = lax.broadcasted_iota(jnp.int32, (chunk, tm), 0) + r0
        onehot = jnp.zeros((chunk, tm), F32)
        for k in range(TOP_K):
            onehot = jnp.where(s == slots[k:k + 1, :], 1.0, onehot)
        xloc[r0:r0 + chunk, :] = _dot(onehot.astype(BF16), h).astype(BF16)

    def copy(local, dst, rows):
        return pltpu.make_async_copy(xloc.at[pl.ds(local, rows)], xs_ref.at[pl.ds(dst, rows)], sem)

    _segment_pieces(tile, seg_start_ref, seg_dst_ref, seg_rows_ref, lambda *a: copy(*a).start())
    _wait_pieces(tile, piece_count_ref, lambda rows: copy(0, 0, rows).wait())


def _dispatch(seg_start, seg_dst, seg_rows, piece_count, h2, slots, xs_buffer):
    T, D = h2.shape
    tm = MOE_TILE
    return pl.pallas_call(
        functools.partial(_dispatch_kernel, chunk=512),
        grid_spec=pltpu.PrefetchScalarGridSpec(
            num_scalar_prefetch=4,
            grid=(T // tm,),
            in_specs=[pl.BlockSpec((tm, D), lambda i, *_: (i, 0)),
                      pl.BlockSpec((TOP_K, tm), lambda i, *_: (0, i)),
                      pl.BlockSpec(memory_space=pl.ANY)],
            out_specs=pl.BlockSpec(memory_space=pl.ANY),
            scratch_shapes=[pltpu.VMEM((LOCAL_SLOTS, D), BF16), pltpu.SemaphoreType.DMA]),
        out_shape=jax.ShapeDtypeStruct(xs_buffer.shape, BF16),
        input_output_aliases={6: 0},
        compiler_params=_cparams(("arbitrary",)),
        name="moe_dispatch",
    )(seg_start, seg_dst, seg_rows, piece_count, h2, slots, xs_buffer)


def _expert_kernel(blk_expert_ref, blk_rows_ref, x_ref, w1_ref, w3_ref, w2_ref, y_ref):
    del blk_expert_ref
    rows = blk_rows_ref[pl.program_id(0)]

    @pl.when(rows > 0)
    def _():
        x = x_ref[...]
        a = _dot(x, w1_ref[0])
        hid = a * jax.nn.sigmoid(a) * _dot(x, w3_ref[0])
        y = _dot(hid.astype(BF16), w2_ref[0])
        r = lax.broadcasted_iota(jnp.int32, y.shape, 0)
        y_ref[...] = jnp.where(r < rows, y, 0.0).astype(BF16)

    @pl.when(rows == 0)
    def _():
        y_ref[...] = jnp.zeros_like(y_ref)


def _experts(blk_expert, blk_rows, xs, w1, w3, w2):
    n_rows, D = xs.shape
    F = EXPERT_FF
    R = EXPERT_BLOCK_ROWS
    return pl.pallas_call(
        _expert_kernel,
        grid_spec=pltpu.PrefetchScalarGridSpec(
            num_scalar_prefetch=2,
            grid=(n_rows // R,),
            in_specs=[pl.BlockSpec((R, D), lambda b, be, br: (b, 0)),
                      pl.BlockSpec((1, D, F), lambda b, be, br: (be[b], 0, 0)),
                      pl.BlockSpec((1, D, F), lambda b, be, br: (be[b], 0, 0)),
                      pl.BlockSpec((1, F, D), lambda b, be, br: (be[b], 0, 0))],
            out_specs=pl.BlockSpec((R, D), lambda b, be, br: (b, 0))),
        out_shape=jax.ShapeDtypeStruct((n_rows, D), BF16),
        compiler_params=_cparams(("arbitrary",)),
        name="moe_experts",
    )(blk_expert, blk_rows, xs, w1, w3, w2)


def _combine_kernel(seg_start_ref, seg_dst_ref, seg_rows_ref, piece_count_ref, ys_ref, slot_ref, gate_ref, h_ref,
                    x_ref, gf_ref, ws1_ref, ws3_ref, ws2_ref, o_ref, yloc, sem, *, chunk):
    tile = pl.program_id(0)
    yloc[...] = jnp.zeros_like(yloc)

    def copy(local, dst, rows):
        return pltpu.make_async_copy(ys_ref.at[pl.ds(dst, rows)], yloc.at[pl.ds(local, rows)], sem)

    _segment_pieces(tile, seg_start_ref, seg_dst_ref, seg_rows_ref, lambda *a: copy(*a).start())
    h = h_ref[...]
    a = _dot(h, ws1_ref[...])
    total = _dot((a * jax.nn.sigmoid(a) * _dot(h, ws3_ref[...])).astype(BF16), ws2_ref[...])
    _wait_pieces(tile, piece_count_ref, lambda rows: copy(0, 0, rows).wait())

    slots = slot_ref[...]
    gates = gate_ref[...]
    tm = h.shape[0]
    for c0 in range(0, yloc.shape[0], chunk):
        s = lax.broadcasted_iota(jnp.int32, (tm, chunk), 1) + c0
        weights = jnp.zeros((tm, chunk), F32)
        for k in range(TOP_K):
            weights = jnp.where(s == slots[:, k:k + 1], gates[:, k:k + 1], weights)
        total = total + _dot(weights.astype(BF16), yloc[c0:c0 + chunk, :])
    o_ref[...] = x_ref[...] + gf_ref[0] * total


def _combine(seg_start, seg_dst, seg_rows, piece_count, ys, slots_tok, gates_tok, h2, x1, g_f, ws1, ws3, ws2, S):
    T, D = x1.shape
    tm = MOE_TILE
    per_batch = S // tm
    F = ws1.shape[1]
    return pl.pallas_call(
        functools.partial(_combine_kernel, chunk=512),
        grid_spec=pltpu.PrefetchScalarGridSpec(
            num_scalar_prefetch=4,
            grid=(T // tm,),
            in_specs=[pl.BlockSpec(memory_space=pl.ANY),
                      pl.BlockSpec((tm, TOP_K), lambda i, *_: (i, 0)),
                      pl.BlockSpec((tm, TOP_K), lambda i, *_: (i, 0)),
                      pl.BlockSpec((tm, D), lambda i, *_: (i, 0)),
                      pl.BlockSpec((tm, D), lambda i, *_: (i, 0)),
                      pl.BlockSpec((1, 1, D), lambda i, *_: (i // per_batch, 0, 0)),
                      pl.BlockSpec((D, F), lambda i, *_: (0, 0)),
                      pl.BlockSpec((D, F), lambda i, *_: (0, 0)),
                      pl.BlockSpec((F, D), lambda i, *_: (0, 0))],
            out_specs=pl.BlockSpec((tm, D), lambda i, *_: (i, 0)),
            scratch_shapes=[pltpu.VMEM((LOCAL_SLOTS, D), BF16), pltpu.SemaphoreType.DMA]),
        out_shape=jax.ShapeDtypeStruct((T, D), F32),
        compiler_params=_cparams(("arbitrary",)),
        name="moe_combine",
    )(seg_start, seg_dst, seg_rows, piece_count, ys, slots_tok, gates_tok, h2, x1, g_f, ws1, ws3, ws2)


def _moe_buffer_rows(T):
    R = EXPERT_BLOCK_ROWS
    n_segments = (T // MOE_TILE) * N_EXPERTS
    return -(-(T * TOP_K + n_segments * (SEG_ALIGN - 1) + N_EXPERTS * (R - SEG_ALIGN)) // R) * R


def _moe(x1, norm_g, sc_f, sh_f, g_f, w_router, b_router, w1, w3, w2, ws1, ws3, ws2, S, xs_buffer):
    T, D = x1.shape
    E = N_EXPERTS
    R = EXPERT_BLOCK_ROWS
    wr_t = w_router.T
    wr_hi = wr_t.astype(BF16)
    wr_lo = (wr_t - wr_hi.astype(F32)).astype(BF16)
    h2, slots, gates, counts = _router(x1, norm_g.reshape(1, D), sc_f, sh_f, wr_hi, wr_lo, b_router.reshape(E, 1), S)

    seg_rows = (counts[:, :, 0] + (SEG_ALIGN - 1)) // SEG_ALIGN * SEG_ALIGN
    seg_start = jnp.cumsum(seg_rows, axis=1) - seg_rows
    expert_rows = jnp.sum(seg_rows, axis=0)
    region = (expert_rows + (R - 1)) // R * R
    region_end = jnp.cumsum(region)
    region_start = region_end - region
    seg_dst = region_start[None, :] + jnp.cumsum(seg_rows, axis=0) - seg_rows
    blk_first = jnp.arange(xs_buffer.shape[0] // R, dtype=jnp.int32) * R
    blk_expert = jnp.minimum(jnp.sum(blk_first[:, None] >= region_end[None, :], axis=1), E - 1).astype(jnp.int32)
    blk_rows = jnp.clip(expert_rows[blk_expert] - (blk_first - region_start[blk_expert]), 0, R).astype(jnp.int32)

    piece_count = jnp.stack([jnp.sum((seg_rows & size) != 0, axis=1) for size in SEG_PIECES], axis=1)

    flat = lambda a: a.reshape(-1).astype(jnp.int32)
    tables = (flat(seg_start), flat(seg_dst), flat(seg_rows), flat(piece_count))
    xs = _dispatch(*tables, h2, slots, xs_buffer)
    ys = _experts(blk_expert, blk_rows, xs, w1.astype(BF16), w3.astype(BF16), w2.astype(BF16))
    out = _combine(*tables, ys, slots.T, gates.T, h2, x1, g_f,
                   ws1.astype(BF16), ws3.astype(BF16), ws2.astype(BF16), S)
    return out, xs


def kernel(x, c, positions, w_mod, b_mod, norm_mix_g, norm_ffn_g, w_in, qn_g, kn_g, lam_q1, lam_k1, lam_q2, lam_k2, subln_g, w_proj_a, w_dw, b_dw, conv_ln_g, conv_ln_b, w_proj_b, b_proj_b, w_proj_c, w_out, w_router, b_router, w1, w3, w2, ws1, ws3, ws2):
    B, S, D = x.shape
    T = B * S
    depth = w_mod.shape[0]
    xt = x.reshape(T, D)

    inv = ROPE_THETA ** (-jnp.arange(0, DA_HEAD_DIM, 2, dtype=F32) / DA_HEAD_DIM)
    inv_lanes = jnp.tile(inv, LANES // (DA_HEAD_DIM // 2)).reshape(1, LANES)
    pos_lanes = jnp.broadcast_to(positions.astype(F32).reshape(T, 1), (T, LANES))
    cos, sin = _rope_tables(pos_lanes, inv_lanes)

    mod = _modulation(c, w_mod, b_mod)

    seg = jnp.arange(DA_WIDTH, dtype=jnp.int32) // DA_HEAD_DIM
    bd = jnp.where(seg[:, None] == seg[None, :], 1.0 / DA_HEAD_DIM, 0.0).astype(BF16)
    kk = jnp.arange(SB_BLOCK, dtype=jnp.int32)
    u = jnp.where(kk[:, None] >= kk[None, :], -1.0, 0.0).astype(BF16)
    u = jnp.concatenate([u, u], axis=0)
    n_seg = DA_WIDTH // DA_HEAD_DIM

    xs_buffer = jnp.zeros((_moe_buffer_rows(T), D), BF16)
    for l in range(depth):
        lambda_init = 0.8 - 0.6 * math.exp(-0.3 * l)
        sh_m, sc_m, g_m, sh_f, sc_f, g_f = [m.reshape(B, 1, D) for m in jnp.split(mod[l], 6, axis=-1)]

        proj = _norm_proj(xt, norm_mix_g[l].reshape(1, D), sc_m, sh_m, w_in[l].astype(BF16), S)

        q_r, k_r = _qk_prep(proj, cos, sin, bd,
                            jnp.tile(qn_g[l], n_seg).reshape(1, DA_WIDTH),
                            jnp.tile(kn_g[l], n_seg).reshape(1, DA_WIDTH))
        lam = (jnp.exp(jnp.sum(lam_q1[l] * lam_k1[l])) - jnp.exp(jnp.sum(lam_q2[l] * lam_k2[l])) + lambda_init)
        score_bound = (DA_HEAD_DIM ** 0.5) * jnp.max(jnp.abs(qn_g[l])) * jnp.max(jnp.abs(kn_g[l]))
        attn = functools.partial(_diff_attn, B=B, S=S, out_scale=1.0 - lambda_init)
        o_a = lax.cond(score_bound <= MAX_FIXED_SHIFT,
                       functools.partial(attn, online=False), functools.partial(attn, online=True),
                       score_bound.reshape(1, 1).astype(F32), jnp.full((1, LANES), lam, F32), q_r, k_r, proj,
                       subln_g[l].reshape(1, LANES))
        cb = _conv_module(proj, w_dw[l], b_dw[l].reshape(1, CONV_CH), conv_ln_g[l].reshape(1, CONV_CH),
                          conv_ln_b[l].reshape(1, CONV_CH), S)
        o_c = _sb_attn(proj, u, B, S)
        x1 = _merge(o_a, cb, o_c, proj, xt, g_m, w_proj_a[l].astype(BF16), w_proj_b[l].astype(BF16),
                    b_proj_b[l].reshape(1, D), w_proj_c[l].astype(BF16), w_out[l].astype(BF16), S)

        xt, xs_buffer = _moe(x1, norm_ffn_g[l], sc_f, sh_f, g_f, w_router[l], b_router[l],
                             w1[l], w3[l], w2[l], ws1[l], ws3[l], ws2[l], S, xs_buffer)

    return xt.reshape(B, S, D)
```

```python
import functools
import math

import jax
import jax.numpy as jnp
from jax import lax
from jax.experimental import pallas as pl
from jax.experimental.pallas import tpu as pltpu

F32 = jnp.float32
BF16 = jnp.bfloat16

D_MODEL = 1024
DA_HEADS = 4
DA_HEAD_DIM = 64
DA_WIDTH = 512
CONV_CH = 512
CONV_WIDTH = 31
SB_WIDTH = 512
SB_HEAD_DIM = 64
IN_COLS = 7168
ROPE_THETA = 10000.0
EPS = 1e-6
NEG_INF = -1e30
LOG2E = math.log2(math.e)
MAX_FIXED_SHIFT = 40.0
SB_BLOCK = 256
SB_BLOCKS_PER_TRIP = 4
N_EXPERTS = 64
TOP_K = 8
N_GROUPS = 8
TOPK_GROUPS = 4
GROUP_SIZE = N_EXPERTS // N_GROUPS
EXPERT_FF = 256
ROUTED_SCALE = 2.5

MOE_TILE = 512
SEG_ALIGN = 16
SEG_PIECES = tuple(SEG_ALIGN << j for j in reversed(range((MOE_TILE // SEG_ALIGN).bit_length())))
LOCAL_SLOTS = -(-(MOE_TILE * TOP_K + N_EXPERTS * (SEG_ALIGN - 1)) // 512) * 512
SEG_RARE = 128
EXPERT_BLOCK_ROWS = 512
LANES = 128
SUBLANES = 8
CONV_HALO = 32

COL_QA, COL_KA, COL_VA, COL_UB, COL_QC, COL_KC, COL_VC, COL_GATES = 0, 512, 1024, 1536, 2560, 3072, 3584, 4096

VMEM_LIMIT = 48 * 1024 * 1024


def _cparams(sem):
    return pltpu.CompilerParams(dimension_semantics=sem, vmem_limit_bytes=VMEM_LIMIT)


def _dot(a, b):
    return jnp.dot(a, b, preferred_element_type=F32)


def _dot_nt(a, b):
    return lax.dot_general(a, b, (((1,), (1,)), ((), ())), preferred_element_type=F32)


def _split_hi_lo(x):
    hi = x.astype(BF16)
    lo = (x - hi.astype(F32)).astype(BF16)
    return hi, lo


def _mod_kernel(c_ref, w_ref, b_ref, o_ref):
    c = c_ref[...]
    c_act = c * jax.nn.sigmoid(c)
    o_ref[0] = jnp.dot(c_act, w_ref[0], precision=lax.Precision.HIGHEST,
                       preferred_element_type=F32) + b_ref[0]


def _modulation(c, w_mod, b_mod):
    L, D, N = w_mod.shape
    B = c.shape[0]
    tn = 1536
    return pl.pallas_call(
        _mod_kernel,
        grid=(L, N // tn),
        in_specs=[pl.BlockSpec((B, D), lambda l, j: (0, 0)),
                  pl.BlockSpec((1, D, tn), lambda l, j: (l, 0, j)),
                  pl.BlockSpec((1, 1, tn), lambda l, j: (l, 0, j))],
        out_specs=pl.BlockSpec((1, B, tn), lambda l, j: (l, 0, j)),
        out_shape=jax.ShapeDtypeStruct((L, B, N), F32),
        compiler_params=_cparams(("arbitrary", "arbitrary")),
        name="modulation",
    )(c, w_mod, b_mod.reshape(L, 1, N))


def _rope_kernel(pos_ref, inv_ref, cos_ref, sin_ref):
    ang = pos_ref[...] * inv_ref[...]
    lane = lax.broadcasted_iota(jnp.int32, ang.shape, 1)
    first_half = (lane % DA_HEAD_DIM) < (DA_HEAD_DIM // 2)
    cos_ref[...] = jnp.cos(ang)
    s = jnp.sin(ang)
    sin_ref[...] = jnp.where(first_half, -s, s)


def _rope_tables(pos_lanes, inv_lanes):
    T = pos_lanes.shape[0]
    tm = 2048
    return pl.pallas_call(
        _rope_kernel,
        grid=(T // tm,),
        in_specs=[pl.BlockSpec((tm, LANES), lambda i: (i, 0)),
                  pl.BlockSpec((1, LANES), lambda i: (0, 0))],
        out_specs=[pl.BlockSpec((tm, LANES), lambda i: (i, 0))] * 2,
        out_shape=[jax.ShapeDtypeStruct((T, LANES), F32)] * 2,
        compiler_params=_cparams(("arbitrary",)),
        name="rope_tables",
    )(pos_lanes, inv_lanes)


def _ada_rms(x, g, sc, sh):
    ms = jnp.mean(x * x, axis=-1, keepdims=True)
    return x * lax.rsqrt(ms + EPS) * g * (1.0 + sc) + sh


def _norm_proj_kernel(x_ref, g_ref, sc_ref, sh_ref, w_ref, o_ref, h_scr):
    @pl.when(pl.program_id(1) == 0)
    def _():
        h_scr[...] = _ada_rms(x_ref[...], g_ref[...], sc_ref[0], sh_ref[0]).astype(BF16)

    o_ref[...] = _dot(h_scr[...], w_ref[...]).astype(BF16)


def _norm_proj(x, g, sc, sh, w_bf16, S):
    T, D = x.shape
    N = w_bf16.shape[1]
    tm, tn = 1024, 1024
    per_batch = S // tm
    return pl.pallas_call(
        _norm_proj_kernel,
        grid=(T // tm, N // tn),
        in_specs=[pl.BlockSpec((tm, D), lambda i, j: (i, 0)),
                  pl.BlockSpec((1, D), lambda i, j: (0, 0)),
                  pl.BlockSpec((1, 1, D), lambda i, j: (i // per_batch, 0, 0)),
                  pl.BlockSpec((1, 1, D), lambda i, j: (i // per_batch, 0, 0)),
                  pl.BlockSpec((D, tn), lambda i, j: (0, j))],
        out_specs=pl.BlockSpec((tm, tn), lambda i, j: (i, j)),
        out_shape=jax.ShapeDtypeStruct((T, N), BF16),
        scratch_shapes=[pltpu.VMEM((tm, D), BF16)],
        compiler_params=_cparams(("arbitrary", "arbitrary")),
        name="norm_proj",
    )(x, g, sc, sh, w_bf16)


def _qk_prep_kernel(q_ref, k_ref, cos_ref, sin_ref, bd_ref, qg_ref, kg_ref, qo_ref, ko_ref):
    cos = cos_ref[...]
    sin = sin_ref[...]
    bd = bd_ref[...]
    lane = lax.broadcasted_iota(jnp.int32, cos.shape, 1)
    first_half = (lane % DA_HEAD_DIM) < (DA_HEAD_DIM // 2)
    half = DA_HEAD_DIM // 2

    def prep(x_ref, g_ref, o_ref, scale):
        x = x_ref[...].astype(F32)
        hi, lo = _split_hi_lo(x * x)
        ms = _dot(hi, bd) + _dot(lo, bd)
        y = x * lax.rsqrt(ms + EPS) * g_ref[...]
        for c in range(DA_WIDTH // LANES):
            yc = y[:, c * LANES:(c + 1) * LANES]
            swapped = jnp.where(first_half, pltpu.roll(yc, LANES - half, 1), pltpu.roll(yc, half, 1))
            o_ref[:, c * LANES:(c + 1) * LANES] = ((yc * cos + swapped * sin) * scale).astype(BF16)

    prep(q_ref, qg_ref, qo_ref, DA_HEAD_DIM ** -0.5)
    prep(k_ref, kg_ref, ko_ref, 1.0)


def _qk_prep(proj, cos, sin, bd, qg, kg):
    T = proj.shape[0]
    tm = 1024
    W = DA_WIDTH
    return pl.pallas_call(
        _qk_prep_kernel,
        grid=(T // tm,),
        in_specs=[pl.BlockSpec((tm, W), lambda i: (i, COL_QA // W)),
                  pl.BlockSpec((tm, W), lambda i: (i, COL_KA // W)),
                  pl.BlockSpec((tm, LANES), lambda i: (i, 0)),
                  pl.BlockSpec((tm, LANES), lambda i: (i, 0)),
                  pl.BlockSpec((W, W), lambda i: (0, 0)),
                  pl.BlockSpec((1, W), lambda i: (0, 0)),
                  pl.BlockSpec((1, W), lambda i: (0, 0))],
        out_specs=[pl.BlockSpec((tm, W), lambda i: (i, 0))] * 2,
        out_shape=[jax.ShapeDtypeStruct((T, W), BF16)] * 2,
        compiler_params=_cparams(("arbitrary",)),
        name="qk_prep",
    )(proj, proj, cos, sin, bd, qg, kg)


def _diff_attn_kernel(shift_ref, lam_ref, q_ref, k_ref, v_ref, g_ref, o_ref, acc1, acc2, *, tq, out_scale, online):
    qi = pl.program_id(2)
    qf = q_ref[...].astype(F32)
    lane = lax.broadcasted_iota(jnp.int32, qf.shape, 1)
    qs = (jnp.where(lane < DA_HEAD_DIM, qf, 0.0).astype(BF16),
          jnp.where(lane >= DA_HEAD_DIM, qf, 0.0).astype(BF16))
    accs = (acc1, acc2)
    acc1[...] = jnp.zeros_like(acc1)
    acc2[...] = jnp.zeros_like(acc2)
    row = lax.broadcasted_iota(jnp.int32, (tq, tq), 0)
    col = lax.broadcasted_iota(jnp.int32, (tq, tq), 1)
    causal = col <= row
    shift = shift_ref[...]

    def step(kj, carry, masked):
        start = pl.multiple_of(kj * tq, tq)
        k = k_ref[pl.ds(start, tq), :]
        v = v_ref[pl.ds(start, tq), :]
        ss = [_dot_nt(q, k) for q in qs]
        if masked:
            ss = [jnp.where(causal, s, NEG_INF) for s in ss]
        if online:
            ms, ls = carry
            m_new = [jnp.maximum(m, jnp.max(s, axis=1, keepdims=True)) for m, s in zip(ms, ss)]
            ps = [jnp.exp(s - m) for s, m in zip(ss, m_new)]
            alphas = [jnp.exp(m - mn) for m, mn in zip(ms, m_new)]
            for acc, alpha, p in zip(accs, alphas, ps):
                acc[...] = alpha * acc[...] + _dot(p.astype(BF16), v)
            ls = [alpha * l + jnp.sum(p, axis=1, keepdims=True) for alpha, l, p in zip(alphas, ls, ps)]
            return tuple(m_new), tuple(ls)
        ps = [jnp.exp(s - shift) for s in ss]
        for acc, p in zip(accs, ps):
            acc[...] += _dot(p.astype(BF16), v)
        return tuple(l + jnp.sum(p, axis=1, keepdims=True) for l, p in zip(carry, ps))

    zero = jnp.zeros((tq, 1), F32)
    if online:
        neg = jnp.full((tq, 1), NEG_INF, F32)
        init = ((neg, neg), (zero, zero))
    else:
        init = (zero, zero)
    carry = lax.fori_loop(0, qi, lambda kj, c: step(kj, c, False), init)
    carry = step(qi, carry, True)
    l1, l2 = carry[1] if online else carry
    o = acc1[...] / l1 - lam_ref[...] * (acc2[...] / l2)
    ms = jnp.mean(o * o, axis=-1, keepdims=True)
    o_ref[...] = (o * lax.rsqrt(ms + EPS) * g_ref[...] * out_scale).astype(BF16)


def _diff_attn(shift, lam_lanes, q_r, k_r, proj, subln_g, B, S, out_scale, online):
    T = q_r.shape[0]
    tq = 512
    nq = S // tq
    return pl.pallas_call(
        functools.partial(_diff_attn_kernel, tq=tq, out_scale=out_scale, online=online),
        grid=(B, DA_HEADS, nq),
        in_specs=[pl.BlockSpec((1, 1), lambda b, h, i: (0, 0)),
                  pl.BlockSpec((1, LANES), lambda b, h, i: (0, 0)),
                  pl.BlockSpec((tq, LANES), lambda b, h, i: (b * nq + i, h)),
                  pl.BlockSpec((S, LANES), lambda b, h, i: (b, h)),
                  pl.BlockSpec((S, LANES), lambda b, h, i: (b, COL_VA // LANES + h)),
                  pl.BlockSpec((1, LANES), lambda b, h, i: (0, 0))],
        out_specs=pl.BlockSpec((tq, LANES), lambda b, h, i: (b * nq + i, h)),
        out_shape=jax.ShapeDtypeStruct((T, DA_WIDTH), BF16),
        scratch_shapes=[pltpu.VMEM((tq, LANES), F32), pltpu.VMEM((tq, LANES), F32)],
        compiler_params=_cparams(("arbitrary", "arbitrary", "arbitrary")),
        name="diff_attn_online" if online else "diff_attn",
    )(shift, lam_lanes, q_r, k_r, proj, subln_g)


def _sb_attn_kernel(q_ref, k_ref, v_ref, u_ref, o_ref, acc, *, tq):
    qi = pl.program_id(2)
    qf = q_ref[...].astype(F32) * (SB_HEAD_DIM ** -0.5 * LOG2E)
    lane = lax.broadcasted_iota(jnp.int32, qf.shape, 1)
    qs = (jnp.where(lane < SB_HEAD_DIM, qf, 0.0).astype(BF16),
          jnp.where(lane >= SB_HEAD_DIM, qf, 0.0).astype(BF16))
    uu = u_ref[...]
    acc[...] = jnp.zeros_like(acc)
    row = lax.broadcasted_iota(jnp.int32, (tq, tq), 0)
    col = lax.broadcasted_iota(jnp.int32, (tq, tq), 1)
    before = col < row

    def step(kjs, rs, masked):
        ks = [k_ref[pl.ds(pl.multiple_of(kj * tq, tq), tq), :] for kj in kjs]
        vs = [v_ref[pl.ds(pl.multiple_of(kj * tq, tq), tq), :] for kj in kjs]
        chains = [(h, b) for b in range(len(kjs)) for h in range(2)]
        zs = [_dot_nt(qs[h], ks[b]) for h, b in chains]
        ps, hls = [], []
        for z in zs:
            neg_abs = pltpu.bitcast(pltpu.bitcast(z, jnp.uint32) | jnp.uint32(0x80000000), F32)
            p = jnp.maximum(z, 0.0) + jnp.log2(1.0 + jnp.exp2(neg_abs))
            if masked:
                p = jnp.where(before, p, 0.0)
            hi, lo = _split_hi_lo(p)
            ps.append(jnp.sum(p, axis=1, keepdims=True))
            hls.append(jnp.concatenate([hi, lo], axis=1))
        mms = [_dot(hl, uu) for hl in hls]
        rs = list(rs)
        for c, (h, b) in enumerate(chains):
            a = jnp.exp2(zs[c] + mms[c] + rs[h])
            if masked:
                a = jnp.where(before, a, 0.0)
            acc[h * tq:(h + 1) * tq, :] += _dot(a.astype(BF16), vs[b])
            rs[h] = rs[h] - ps[c]
        return tuple(rs)

    zero = jnp.zeros((tq, 1), F32)
    rs = step([qi], (zero, zero), True)
    def run(first, count, c):
        return step([first - d for d in range(count)], c, False)

    nb = SB_BLOCKS_PER_TRIP
    rs = lax.fori_loop(0, qi // nb, lambda t, c: run(qi - 1 - nb * t, nb, c), rs)
    left = qi % nb
    part = nb // 2
    while part >= 1:
        rs = lax.cond(left & part != 0, functools.partial(run, (left & (2 * part - 1)) - 1, part),
                      lambda c: c, rs)
        part //= 2

    o_ref[...] = jnp.where(lane < SB_HEAD_DIM, acc[0:tq, :], acc[tq:2 * tq, :]).astype(BF16)


def _sb_attn(proj, u, B, S):
    T = proj.shape[0]
    tq = SB_BLOCK
    nq = S // tq
    return pl.pallas_call(
        functools.partial(_sb_attn_kernel, tq=tq),
        grid=(B, SB_WIDTH // LANES, nq),
        in_specs=[pl.BlockSpec((tq, LANES), lambda b, p, i: (b * nq + i, COL_QC // LANES + p)),
                  pl.BlockSpec((S, LANES), lambda b, p, i: (b, COL_KC // LANES + p)),
                  pl.BlockSpec((S, LANES), lambda b, p, i: (b, COL_VC // LANES + p)),
                  pl.BlockSpec((2 * tq, tq), lambda b, p, i: (0, 0))],
        out_specs=pl.BlockSpec((tq, LANES), lambda b, p, i: (b * nq + i, p)),
        out_shape=jax.ShapeDtypeStruct((T, SB_WIDTH), BF16),
        scratch_shapes=[pltpu.VMEM((2 * tq, LANES), F32)],
        compiler_params=_cparams(("arbitrary", "arbitrary", "arbitrary")),
        name="sb_attn",
    )(proj, proj, proj, u)


def _conv_kernel(a_ref, g_ref, ah_ref, gh_ref, w_ref, b_ref, lg_ref, lb_ref, o_ref, hbuf, *, tm, per_batch, rows):
    i = pl.program_id(0)
    halo = ah_ref[...].astype(F32) * jax.nn.sigmoid(gh_ref[...].astype(F32))
    hbuf[0:CONV_HALO, :] = jnp.where(i % per_batch == 0, 0.0, halo)
    hbuf[CONV_HALO:, :] = a_ref[...].astype(F32) * jax.nn.sigmoid(g_ref[...].astype(F32))
    w = w_ref[...]
    for r0 in range(0, tm, rows):
        acc = jnp.broadcast_to(b_ref[...], (rows, CONV_CH))
        for b in range(SUBLANES):
            part = None
            for a in range((CONV_WIDTH - 1 - b) // SUBLANES + 1):
                lo = r0 + CONV_HALO - SUBLANES - SUBLANES * a
                tap = CONV_WIDTH - 1 - SUBLANES * a - b
                term = hbuf[lo:lo + rows + SUBLANES, :] * w[tap:tap + 1, :]
                part = term if part is None else part + term
            acc = acc + part[SUBLANES - b:SUBLANES - b + rows, :]
        mu = jnp.mean(acc, axis=-1, keepdims=True)
        d = acc - mu
        var = jnp.mean(d * d, axis=-1, keepdims=True)
        y = d * lax.rsqrt(var + EPS) * lg_ref[...] + lb_ref[...]
        o_ref[r0:r0 + rows, :] = (y * jax.nn.sigmoid(y)).astype(BF16)


def _conv_module(proj, w_dw, b_dw, ln_g, ln_b, S):
    T = proj.shape[0]
    tm = 512
    per_batch = S // tm
    C = CONV_CH
    ca, cg = COL_UB // C, COL_UB // C + 1
    halo_blocks = tm // CONV_HALO

    def halo_idx(col):
        return lambda i: (jnp.maximum(i * halo_blocks - 1, 0), col)

    return pl.pallas_call(
        functools.partial(_conv_kernel, tm=tm, per_batch=per_batch, rows=128),
        grid=(T // tm,),
        in_specs=[pl.BlockSpec((tm, C), lambda i: (i, ca)),
                  pl.BlockSpec((tm, C), lambda i: (i, cg)),
                  pl.BlockSpec((CONV_HALO, C), halo_idx(ca)),
                  pl.BlockSpec((CONV_HALO, C), halo_idx(cg)),
                  pl.BlockSpec((CONV_WIDTH, C), lambda i: (0, 0)),
                  pl.BlockSpec((1, C), lambda i: (0, 0)),
                  pl.BlockSpec((1, C), lambda i: (0, 0)),
                  pl.BlockSpec((1, C), lambda i: (0, 0))],
        out_specs=pl.BlockSpec((tm, C), lambda i: (i, 0)),
        out_shape=jax.ShapeDtypeStruct((T, C), BF16),
        scratch_shapes=[pltpu.VMEM((tm + CONV_HALO, C), F32)],
        compiler_params=_cparams(("arbitrary",)),
        name="conv_module",
    )(proj, proj, proj, proj, w_dw, b_dw, ln_g, ln_b)


def _merge_kernel(oa_ref, cb_ref, oc_ref, g0_ref, g1_ref, g2_ref, x_ref, gm_ref,
                  wa_ref, wb_ref, bb_ref, wc_ref, wo_ref, o_ref):
    y_a = _dot(oa_ref[...], wa_ref[...])
    y_b = _dot(cb_ref[...], wb_ref[...]) + bb_ref[...]
    y_c = _dot(oc_ref[...], wc_ref[...])
    merged = (jax.nn.sigmoid(g0_ref[...].astype(F32)) * y_a
              + jax.nn.sigmoid(g1_ref[...].astype(F32)) * y_b
              + jax.nn.sigmoid(g2_ref[...].astype(F32)) * y_c)
    o_ref[...] = x_ref[...] + gm_ref[0] * _dot(merged.astype(BF16), wo_ref[...])


def _merge(o_a, cb, o_c, proj, x, g_m, wa, wb, bb, wc, wo, S):
    T, D = x.shape
    tm = 512
    per_batch = S // tm
    W = DA_WIDTH
    gcol = COL_GATES // D
    branch = pl.BlockSpec((tm, W), lambda i: (i, 0))
    wspec = pl.BlockSpec((W, D), lambda i: (0, 0))
    return pl.pallas_call(
        _merge_kernel,
        grid=(T // tm,),
        in_specs=[branch, branch, branch,
                  pl.BlockSpec((tm, D), lambda i: (i, gcol)),
                  pl.BlockSpec((tm, D), lambda i: (i, gcol + 1)),
                  pl.BlockSpec((tm, D), lambda i: (i, gcol + 2)),
                  pl.BlockSpec((tm, D), lambda i: (i, 0)),
                  pl.BlockSpec((1, 1, D), lambda i: (i // per_batch, 0, 0)),
                  wspec, wspec, pl.BlockSpec((1, D), lambda i: (0, 0)), wspec,
                  pl.BlockSpec((D, D), lambda i: (0, 0))],
        out_specs=pl.BlockSpec((tm, D), lambda i: (i, 0)),
        out_shape=jax.ShapeDtypeStruct((T, D), F32),
        compiler_params=_cparams(("arbitrary",)),
        name="merge",
    )(o_a, cb, o_c, proj, proj, proj, x, g_m, wa, wb, bb, wc, wo)


def _first_index_of_max(vals, idx, size):
    mx = jnp.max(vals, axis=0, keepdims=True)
    first = jnp.min(jnp.where(vals == mx, idx, size), axis=0, keepdims=True)
    return mx, first


def _router_kernel(x_ref, g_ref, sc_ref, sh_ref, wh_ref, wl_ref, b_ref, before_ref, lower_ref,
                   h_ref, slot_ref, gate_ref, count_ref):
    h = _ada_rms(x_ref[...], g_ref[...], sc_ref[0], sh_ref[0])
    h_ref[...] = h.astype(BF16)
    h_hi, h_lo = _split_hi_lo(h)
    logits = _dot_nt(wh_ref[...], h_hi) + _dot_nt(wh_ref[...], h_lo) + _dot_nt(wl_ref[...], h_hi)
    scores = jax.nn.sigmoid(logits)
    biased = scores + b_ref[...]
    tm = scores.shape[1]
    minus_inf = -jnp.inf

    in_group = lax.broadcasted_iota(jnp.int32, (GROUP_SIZE, tm), 0)
    group_scores = []
    for g in range(N_GROUPS):
        vals = biased[g * GROUP_SIZE:(g + 1) * GROUP_SIZE]
        m1, i1 = _first_index_of_max(vals, in_group, GROUP_SIZE)
        m2 = jnp.max(jnp.where(in_group == i1, minus_inf, vals), axis=0, keepdims=True)
        group_scores.append(m1 + m2)
    gs = jnp.concatenate(group_scores, axis=0)

    gidx = lax.broadcasted_iota(jnp.int32, (N_GROUPS, tm), 0)
    group_sel = jnp.zeros((N_GROUPS, tm), jnp.bool_)
    for _ in range(TOPK_GROUPS):
        _, first = _first_index_of_max(gs, gidx, N_GROUPS)
        pick = gidx == first
        group_sel = jnp.logical_or(group_sel, pick)
        gs = jnp.where(pick, minus_inf, gs)

    eidx = lax.broadcasted_iota(jnp.int32, (N_EXPERTS, tm), 0)
    expert_group_sel = jnp.concatenate(
        [jnp.broadcast_to(group_sel[g:g + 1], (GROUP_SIZE, tm)) for g in range(N_GROUPS)], axis=0)
    cand = jnp.where(expert_group_sel, biased, NEG_INF)
    picks = []
    for _ in range(TOP_K):
        _, first = _first_index_of_max(cand, eidx, N_EXPERTS)
        pick = eidx == first
        picks.append(pick)
        cand = jnp.where(pick, minus_inf, cand)

    chosen = jnp.zeros((N_EXPERTS, tm), F32)
    for pick in picks:
        chosen = jnp.where(pick, 1.0, chosen)
    rank = _dot(chosen.astype(BF16), before_ref[...])
    count = jnp.sum(chosen, axis=1, keepdims=True).astype(jnp.int32)
    padded = (count + (SEG_ALIGN - 1)) & (-SEG_ALIGN)
    seg_start = _dot(lower_ref[...], jnp.broadcast_to(padded, (N_EXPERTS, LANES)).astype(BF16))[:, 0:1]
    slot_of = seg_start + rank
    slots = [jnp.sum(jnp.where(pick, slot_of, 0.0), axis=0, keepdims=True) for pick in picks]
    ws = [jnp.sum(jnp.where(pick, scores, 0.0), axis=0, keepdims=True) for pick in picks]
    norm = ROUTED_SCALE / (sum(ws) + 1e-20)
    slot_ref[...] = jnp.concatenate(slots, axis=0).astype(jnp.int32)
    gate_ref[...] = jnp.concatenate(ws, axis=0) * norm
    count_ref[0] = jnp.broadcast_to(count, (N_EXPERTS, LANES))


def _router(x1, g, sc, sh, wr_hi, wr_lo, b_router, S):
    T, D = x1.shape
    tm = MOE_TILE
    per_batch = S // tm
    E = N_EXPERTS
    tok = jnp.arange(tm, dtype=jnp.int32)
    before = (tok[:, None] < tok[None, :]).astype(BF16)
    ex = jnp.arange(E, dtype=jnp.int32)
    lower = (ex[None, :] < ex[:, None]).astype(BF16)
    return pl.pallas_call(
        _router_kernel,
        grid=(T // tm,),
        in_specs=[pl.BlockSpec((tm, D), lambda i: (i, 0)),
                  pl.BlockSpec((1, D), lambda i: (0, 0)),
                  pl.BlockSpec((1, 1, D), lambda i: (i // per_batch, 0, 0)),
                  pl.BlockSpec((1, 1, D), lambda i: (i // per_batch, 0, 0)),
                  pl.BlockSpec((E, D), lambda i: (0, 0)),
                  pl.BlockSpec((E, D), lambda i: (0, 0)),
                  pl.BlockSpec((E, 1), lambda i: (0, 0)),
                  pl.BlockSpec((tm, tm), lambda i: (0, 0)),
                  pl.BlockSpec((E, E), lambda i: (0, 0))],
        out_specs=[pl.BlockSpec((tm, D), lambda i: (i, 0)),
                   pl.BlockSpec((TOP_K, tm), lambda i: (0, i)),
                   pl.BlockSpec((TOP_K, tm), lambda i: (0, i)),
                   pl.BlockSpec((1, E, LANES), lambda i: (i, 0, 0))],
        out_shape=[jax.ShapeDtypeStruct((T, D), BF16),
                   jax.ShapeDtypeStruct((TOP_K, T), jnp.int32),
                   jax.ShapeDtypeStruct((TOP_K, T), F32),
                   jax.ShapeDtypeStruct((T // tm, E, LANES), jnp.int32)],
        compiler_params=_cparams(("arbitrary",)),
        name="router",
    )(x1, g, sc, sh, wr_hi, wr_lo, b_router, before, lower)


def _segment_pieces(tile, seg_start_ref, seg_dst_ref, seg_rows_ref, fn):
    def body(e, carry):
        idx = tile * N_EXPERTS + e
        local, dst, rows = seg_start_ref[idx], seg_dst_ref[idx], seg_rows_ref[idx]

        def pieces(sizes):
            for size in sizes:
                @pl.when(rows & size != 0)
                def _():
                    done = rows & (-2 * size)
                    fn(pl.multiple_of(local + done, SEG_ALIGN), pl.multiple_of(dst + done, SEG_ALIGN), size)

        pieces([s for s in SEG_PIECES if s < SEG_RARE])
        pl.when(rows >= SEG_RARE)(lambda: pieces([s for s in SEG_PIECES if s >= SEG_RARE]))
        return carry

    lax.fori_loop(0, N_EXPERTS, body, 0)


def _wait_pieces(tile, piece_count_ref, wait_one):
    for j, size in enumerate(SEG_PIECES):
        def body(_, carry, size=size):
            wait_one(size)
            return carry

        lax.fori_loop(0, piece_count_ref[tile * len(SEG_PIECES) + j], body, 0)


def _dispatch_kernel(seg_start_ref, seg_dst_ref, seg_rows_ref, piece_count_ref, h_ref, slot_ref, xs_in_ref, xs_ref,
                     xloc, sem, *, chunk):
    del xs_in_ref
    tile = pl.program_id(0)
    buf = tile % 2
    slots = slot_ref[...]
    h = h_ref[...]
    tm = h.shape[0]
    for r0 in range(0, xloc.shape[1], chunk):
        s = lax.broadcasted_iota(jnp.int32, (chunk, tm), 0) + r0
        onehot = jnp.zeros((chunk, tm), F32)
        for k in range(TOP_K):
            onehot = jnp.where(s == slots[k:k + 1, :], 1.0, onehot)
        xloc[buf, r0:r0 + chunk, :] = _dot(onehot.astype(BF16), h).astype(BF16)

    def copy(b, local, dst, rows):
        return pltpu.make_async_copy(xloc.at[b, pl.ds(local, rows)], xs_ref.at[pl.ds(dst, rows)], sem.at[b])

    _segment_pieces(tile, seg_start_ref, seg_dst_ref, seg_rows_ref, lambda *a: copy(buf, *a).start())

    @pl.when(tile > 0)
    def _():
        _wait_pieces(tile - 1, piece_count_ref, lambda rows: copy(1 - buf, 0, 0, rows).wait())

    @pl.when(tile == pl.num_programs(0) - 1)
    def _():
        _wait_pieces(tile, piece_count_ref, lambda rows: copy(buf, 0, 0, rows).wait())


def _dispatch(seg_start, seg_dst, seg_rows, piece_count, h2, slots, xs_buffer):
    T, D = h2.shape
    tm = MOE_TILE
    return pl.pallas_call(
        functools.partial(_dispatch_kernel, chunk=512),
        grid_spec=pltpu.PrefetchScalarGridSpec(
            num_scalar_prefetch=4,
            grid=(T // tm,),
            in_specs=[pl.BlockSpec((tm, D), lambda i, *_: (i, 0)),
                      pl.BlockSpec((TOP_K, tm), lambda i, *_: (0, i)),
                      pl.BlockSpec(memory_space=pl.ANY)],
            out_specs=pl.BlockSpec(memory_space=pl.ANY),
            scratch_shapes=[pltpu.VMEM((2, LOCAL_SLOTS, D), BF16), pltpu.SemaphoreType.DMA((2,))]),
        out_shape=jax.ShapeDtypeStruct(xs_buffer.shape, BF16),
        input_output_aliases={6: 0},
        compiler_params=_cparams(("arbitrary",)),
        name="moe_dispatch",
    )(seg_start, seg_dst, seg_rows, piece_count, h2, slots, xs_buffer)


def _expert_kernel(blk_expert_ref, blk_rows_ref, x_ref, w1_ref, w3_ref, w2_ref, y_ref):
    del blk_expert_ref
    rows = blk_rows_ref[pl.program_id(0)]

    @pl.when(rows > 0)
    def _():
        x = x_ref[...]
        a = _dot(x, w1_ref[0])
        hid = a * jax.nn.sigmoid(a) * _dot(x, w3_ref[0])
        y = _dot(hid.astype(BF16), w2_ref[0])
        r = lax.broadcasted_iota(jnp.int32, y.shape, 0)
        y_ref[...] = jnp.where(r < rows, y, 0.0).astype(BF16)

    @pl.when(rows == 0)
    def _():
        y_ref[...] = jnp.zeros_like(y_ref)


def _experts(blk_expert, blk_rows, xs, w1, w3, w2):
    n_rows, D = xs.shape
    F = EXPERT_FF
    R = EXPERT_BLOCK_ROWS
    return pl.pallas_call(
        _expert_kernel,
        grid_spec=pltpu.PrefetchScalarGridSpec(
            num_scalar_prefetch=2,
            grid=(n_rows // R,),
            in_specs=[pl.BlockSpec((R, D), lambda b, be, br: (b, 0)),
                      pl.BlockSpec((1, D, F), lambda b, be, br: (be[b], 0, 0)),
                      pl.BlockSpec((1, D, F), lambda b, be, br: (be[b], 0, 0)),
                      pl.BlockSpec((1, F, D), lambda b, be, br: (be[b], 0, 0))],
            out_specs=pl.BlockSpec((R, D), lambda b, be, br: (b, 0))),
        out_shape=jax.ShapeDtypeStruct((n_rows, D), BF16),
        compiler_params=_cparams(("arbitrary",)),
        name="moe_experts",
    )(blk_expert, blk_rows, xs, w1, w3, w2)


def _combine_kernel(seg_start_ref, seg_dst_ref, seg_rows_ref, piece_count_ref, ys_ref, slot_ref, gate_ref, h_ref,
                    x_ref, gf_ref, ws1_ref, ws3_ref, ws2_ref, o_ref, yloc, sem, *, chunk):
    tile = pl.program_id(0)
    buf = tile % 2

    def copy(b, local, dst, rows):
        return pltpu.make_async_copy(ys_ref.at[pl.ds(dst, rows)], yloc.at[b, pl.ds(local, rows)], sem.at[b])

    def fetch(t, b):
        yloc[b] = jnp.zeros(yloc.shape[1:], BF16)
        _segment_pieces(t, seg_start_ref, seg_dst_ref, seg_rows_ref, lambda *a: copy(b, *a).start())

    pl.when(tile == 0)(lambda: fetch(tile, buf))
    pl.when(tile + 1 < pl.num_programs(0))(lambda: fetch(tile + 1, 1 - buf))
    h = h_ref[...]
    a = _dot(h, ws1_ref[...])
    total = _dot((a * jax.nn.sigmoid(a) * _dot(h, ws3_ref[...])).astype(BF16), ws2_ref[...])
    _wait_pieces(tile, piece_count_ref, lambda rows: copy(buf, 0, 0, rows).wait())

    slots = slot_ref[...]
    gates = gate_ref[...]
    tm = h.shape[0]
    for c0 in range(0, yloc.shape[1], chunk):
        s = lax.broadcasted_iota(jnp.int32, (tm, chunk), 1) + c0
        weights = jnp.zeros((tm, chunk), F32)
        for k in range(TOP_K):
            weights = jnp.where(s == slots[:, k:k + 1], gates[:, k:k + 1], weights)
        total = total + _dot(weights.astype(BF16), yloc[buf, c0:c0 + chunk, :])
    o_ref[...] = x_ref[...] + gf_ref[0] * total


def _combine(seg_start, seg_dst, seg_rows, piece_count, ys, slots_tok, gates_tok, h2, x1, g_f, ws1, ws3, ws2, S):
    T, D = x1.shape
    tm = MOE_TILE
    per_batch = S // tm
    F = ws1.shape[1]
    return pl.pallas_call(
        functools.partial(_combine_kernel, chunk=512),
        grid_spec=pltpu.PrefetchScalarGridSpec(
            num_scalar_prefetch=4,
            grid=(T // tm,),
            in_specs=[pl.BlockSpec(memory_space=pl.ANY),
                      pl.BlockSpec((tm, TOP_K), lambda i, *_: (i, 0)),
                      pl.BlockSpec((tm, TOP_K), lambda i, *_: (i, 0)),
                      pl.BlockSpec((tm, D), lambda i, *_: (i, 0)),
                      pl.BlockSpec((tm, D), lambda i, *_: (i, 0)),
                      pl.BlockSpec((1, 1, D), lambda i, *_: (i // per_batch, 0, 0)),
                      pl.BlockSpec((D, F), lambda i, *_: (0, 0)),
                      pl.BlockSpec((D, F), lambda i, *_: (0, 0)),
                      pl.BlockSpec((F, D), lambda i, *_: (0, 0))],
            out_specs=pl.BlockSpec((tm, D), lambda i, *_: (i, 0)),
            scratch_shapes=[pltpu.VMEM((2, LOCAL_SLOTS, D), BF16), pltpu.SemaphoreType.DMA((2,))]),
        out_shape=jax.ShapeDtypeStruct((T, D), F32),
        compiler_params=_cparams(("arbitrary",)),
        name="moe_combine",
    )(seg_start, seg_dst, seg_rows, piece_count, ys, slots_tok, gates_tok, h2, x1, g_f, ws1, ws3, ws2)


def _moe_buffer_rows(T):
    R = EXPERT_BLOCK_ROWS
    n_segments = (T // MOE_TILE) * N_EXPERTS
    return -(-(T * TOP_K + n_segments * (SEG_ALIGN - 1) + N_EXPERTS * (R - SEG_ALIGN)) // R) * R


def _moe(x1, norm_g, sc_f, sh_f, g_f, w_router, b_router, w1, w3, w2, ws1, ws3, ws2, S, xs_buffer):
    T, D = x1.shape
    E = N_EXPERTS
    R = EXPERT_BLOCK_ROWS
    wr_t = w_router.T
    wr_hi = wr_t.astype(BF16)
    wr_lo = (wr_t - wr_hi.astype(F32)).astype(BF16)
    h2, slots, gates, counts = _router(x1, norm_g.reshape(1, D), sc_f, sh_f, wr_hi, wr_lo, b_router.reshape(E, 1), S)

    seg_rows = (counts[:, :, 0] + (SEG_ALIGN - 1)) // SEG_ALIGN * SEG_ALIGN
    seg_start = jnp.cumsum(seg_rows, axis=1) - seg_rows
    expert_rows = jnp.sum(seg_rows, axis=0)
    region = (expert_rows + (R - 1)) // R * R
    region_end = jnp.cumsum(region)
    region_start = region_end - region
    seg_dst = region_start[None, :] + jnp.cumsum(seg_rows, axis=0) - seg_rows
    blk_first = jnp.arange(xs_buffer.shape[0] // R, dtype=jnp.int32) * R
    blk_expert = jnp.minimum(jnp.sum(blk_first[:, None] >= region_end[None, :], axis=1), E - 1).astype(jnp.int32)
    blk_rows = jnp.clip(expert_rows[blk_expert] - (blk_first - region_start[blk_expert]), 0, R).astype(jnp.int32)

    piece_count = jnp.stack([jnp.sum((seg_rows & size) != 0, axis=1) for size in SEG_PIECES], axis=1)

    flat = lambda a: a.reshape(-1).astype(jnp.int32)
    tables = (flat(seg_start), flat(seg_dst), flat(seg_rows), flat(piece_count))
    xs = _dispatch(*tables, h2, slots, xs_buffer)
    ys = _experts(blk_expert, blk_rows, xs, w1.astype(BF16), w3.astype(BF16), w2.astype(BF16))
    out = _combine(*tables, ys, slots.T, gates.T, h2, x1, g_f,
                   ws1.astype(BF16), ws3.astype(BF16), ws2.astype(BF16), S)
    return out, xs


def kernel(x, c, positions, w_mod, b_mod, norm_mix_g, norm_ffn_g, w_in, qn_g, kn_g, lam_q1, lam_k1, lam_q2, lam_k2, subln_g, w_proj_a, w_dw, b_dw, conv_ln_g, conv_ln_b, w_proj_b, b_proj_b, w_proj_c, w_out, w_router, b_router, w1, w3, w2, ws1, ws3, ws2):
    B, S, D = x.shape
    T = B * S
    depth = w_mod.shape[0]
    xt = x.reshape(T, D)

    inv = ROPE_THETA ** (-jnp.arange(0, DA_HEAD_DIM, 2, dtype=F32) / DA_HEAD_DIM)
    inv_lanes = jnp.tile(inv, LANES // (DA_HEAD_DIM // 2)).reshape(1, LANES)
    pos_lanes = jnp.broadcast_to(positions.astype(F32).reshape(T, 1), (T, LANES))
    cos, sin = _rope_tables(pos_lanes, inv_lanes)

    mod = _modulation(c, w_mod, b_mod)

    seg = jnp.arange(DA_WIDTH, dtype=jnp.int32) // DA_HEAD_DIM
    bd = jnp.where(seg[:, None] == seg[None, :], 1.0 / DA_HEAD_DIM, 0.0).astype(BF16)
    kk = jnp.arange(SB_BLOCK, dtype=jnp.int32)
    u = jnp.where(kk[:, None] >= kk[None, :], -1.0, 0.0).astype(BF16)
    u = jnp.concatenate([u, u], axis=0)
    n_seg = DA_WIDTH // DA_HEAD_DIM

    xs_buffer = jnp.zeros((_moe_buffer_rows(T), D), BF16)
    for l in range(depth):
        lambda_init = 0.8 - 0.6 * math.exp(-0.3 * l)
        sh_m, sc_m, g_m, sh_f, sc_f, g_f = [m.reshape(B, 1, D) for m in jnp.split(mod[l], 6, axis=-1)]

        proj = _norm_proj(xt, norm_mix_g[l].reshape(1, D), sc_m, sh_m, w_in[l].astype(BF16), S)

        q_r, k_r = _qk_prep(proj, cos, sin, bd,
                            jnp.tile(qn_g[l], n_seg).reshape(1, DA_WIDTH),
                            jnp.tile(kn_g[l], n_seg).reshape(1, DA_WIDTH))
        lam = (jnp.exp(jnp.sum(lam_q1[l] * lam_k1[l])) - jnp.exp(jnp.sum(lam_q2[l] * lam_k2[l])) + lambda_init)
        score_bound = (DA_HEAD_DIM ** 0.5) * jnp.max(jnp.abs(qn_g[l])) * jnp.max(jnp.abs(kn_g[l]))
        attn = functools.partial(_diff_attn, B=B, S=S, out_scale=1.0 - lambda_init)
        o_a = lax.cond(score_bound <= MAX_FIXED_SHIFT,
                       functools.partial(attn, online=False), functools.partial(attn, online=True),
                       score_bound.reshape(1, 1).astype(F32), jnp.full((1, LANES), lam, F32), q_r, k_r, proj,
                       subln_g[l].reshape(1, LANES))
        cb = _conv_module(proj, w_dw[l], b_dw[l].reshape(1, CONV_CH), conv_ln_g[l].reshape(1, CONV_CH),
                          conv_ln_b[l].reshape(1, CONV_CH), S)
        o_c = _sb_attn(proj, u, B, S)
        x1 = _merge(o_a, cb, o_c, proj, xt, g_m, w_proj_a[l].astype(BF16), w_proj_b[l].astype(BF16),
                    b_proj_b[l].reshape(1, D), w_proj_c[l].astype(BF16), w_out[l].astype(BF16), S)

        xt, xs_buffer = _moe(x1, norm_ffn_g[l], sc_f, sh_f, g_f, w_router[l], b_router[l],
                             w1[l], w3[l], w2[l], ws1[l], ws3[l], ws2[l], S, xs_buffer)

    return xt.reshape(B, S, D)
```

```python
import functools
import math

import jax
import jax.numpy as jnp
from jax import lax
from jax.experimental import pallas as pl
from jax.experimental.pallas import tpu as pltpu

F32 = jnp.float32
BF16 = jnp.bfloat16

D_MODEL = 1024
DA_HEADS = 4
DA_HEAD_DIM = 64
DA_WIDTH = 512
CONV_CH = 512
CONV_WIDTH = 31
SB_WIDTH = 512
SB_HEAD_DIM = 64
IN_COLS = 7168
ROPE_THETA = 10000.0
EPS = 1e-6
NEG_INF = -1e30
LOG2E = math.log2(math.e)
MAX_FIXED_SHIFT = 40.0
SB_BLOCK = 256
SB_BLOCKS_PER_TRIP = 4
N_EXPERTS = 64
TOP_K = 8
N_GROUPS = 8
TOPK_GROUPS = 4
GROUP_SIZE = N_EXPERTS // N_GROUPS
EXPERT_FF = 256
ROUTED_SCALE = 2.5

MOE_TILE = 512
SEG_ALIGN = 16
SEG_PIECES = tuple(SEG_ALIGN << j for j in reversed(range((MOE_TILE // SEG_ALIGN).bit_length())))
LOCAL_SLOTS = -(-(MOE_TILE * TOP_K + N_EXPERTS * (SEG_ALIGN - 1)) // 512) * 512
SEG_RARE = 128
EXPERT_BLOCK_ROWS = 512
LANES = 128
SUBLANES = 8
CONV_HALO = 32

COL_QA, COL_KA, COL_VA, COL_UB, COL_QC, COL_KC, COL_VC, COL_GATES = 0, 512, 1024, 1536, 2560, 3072, 3584, 4096

VMEM_LIMIT = 48 * 1024 * 1024


def _cparams(sem):
    return pltpu.CompilerParams(dimension_semantics=sem, vmem_limit_bytes=VMEM_LIMIT)


def _dot(a, b):
    return jnp.dot(a, b, preferred_element_type=F32)


def _dot_nt(a, b):
    return lax.dot_general(a, b, (((1,), (1,)), ((), ())), preferred_element_type=F32)


def _split_hi_lo(x):
    hi = x.astype(BF16)
    lo = (x - hi.astype(F32)).astype(BF16)
    return hi, lo


def _mod_kernel(c_ref, w_ref, b_ref, o_ref):
    c = c_ref[...]
    c_act = c * jax.nn.sigmoid(c)
    o_ref[0] = jnp.dot(c_act, w_ref[0], precision=lax.Precision.HIGHEST,
                       preferred_element_type=F32) + b_ref[0]


def _modulation(c, w_mod, b_mod):
    L, D, N = w_mod.shape
    B = c.shape[0]
    tn = 1536
    return pl.pallas_call(
        _mod_kernel,
        grid=(L, N // tn),
        in_specs=[pl.BlockSpec((B, D), lambda l, j: (0, 0)),
                  pl.BlockSpec((1, D, tn), lambda l, j: (l, 0, j)),
                  pl.BlockSpec((1, 1, tn), lambda l, j: (l, 0, j))],
        out_specs=pl.BlockSpec((1, B, tn), lambda l, j: (l, 0, j)),
        out_shape=jax.ShapeDtypeStruct((L, B, N), F32),
        compiler_params=_cparams(("arbitrary", "arbitrary")),
        name="modulation",
    )(c, w_mod, b_mod.reshape(L, 1, N))


def _rope_kernel(pos_ref, inv_ref, cos_ref, sin_ref):
    ang = pos_ref[...] * inv_ref[...]
    lane = lax.broadcasted_iota(jnp.int32, ang.shape, 1)
    first_half = (lane % DA_HEAD_DIM) < (DA_HEAD_DIM // 2)
    cos_ref[...] = jnp.cos(ang)
    s = jnp.sin(ang)
    sin_ref[...] = jnp.where(first_half, -s, s)


def _rope_tables(pos_lanes, inv_lanes):
    T = pos_lanes.shape[0]
    tm = 2048
    return pl.pallas_call(
        _rope_kernel,
        grid=(T // tm,),
        in_specs=[pl.BlockSpec((tm, LANES), lambda i: (i, 0)),
                  pl.BlockSpec((1, LANES), lambda i: (0, 0))],
        out_specs=[pl.BlockSpec((tm, LANES), lambda i: (i, 0))] * 2,
        out_shape=[jax.ShapeDtypeStruct((T, LANES), F32)] * 2,
        compiler_params=_cparams(("arbitrary",)),
        name="rope_tables",
    )(pos_lanes, inv_lanes)


def _ada_rms(x, g, sc, sh):
    ms = jnp.mean(x * x, axis=-1, keepdims=True)
    return x * lax.rsqrt(ms + EPS) * g * (1.0 + sc) + sh


def _norm_proj_kernel(x_ref, g_ref, sc_ref, sh_ref, w_ref, o_ref, h_scr):
    @pl.when(pl.program_id(1) == 0)
    def _():
        h_scr[...] = _ada_rms(x_ref[...], g_ref[...], sc_ref[0], sh_ref[0]).astype(BF16)

    o_ref[...] = _dot(h_scr[...], w_ref[...]).astype(BF16)


def _norm_proj(x, g, sc, sh, w_bf16, S):
    T, D = x.shape
    N = w_bf16.shape[1]
    tm, tn = 1024, 1024
    per_batch = S // tm
    return pl.pallas_call(
        _norm_proj_kernel,
        grid=(T // tm, N // tn),
        in_specs=[pl.BlockSpec((tm, D), lambda i, j: (i, 0)),
                  pl.BlockSpec((1, D), lambda i, j: (0, 0)),
                  pl.BlockSpec((1, 1, D), lambda i, j: (i // per_batch, 0, 0)),
                  pl.BlockSpec((1, 1, D), lambda i, j: (i // per_batch, 0, 0)),
                  pl.BlockSpec((D, tn), lambda i, j: (0, j))],
        out_specs=pl.BlockSpec((tm, tn), lambda i, j: (i, j)),
        out_shape=jax.ShapeDtypeStruct((T, N), BF16),
        scratch_shapes=[pltpu.VMEM((tm, D), BF16)],
        compiler_params=_cparams(("arbitrary", "arbitrary")),
        name="norm_proj",
    )(x, g, sc, sh, w_bf16)


def _qk_prep_kernel(q_ref, k_ref, cos_ref, sin_ref, bd_ref, qg_ref, kg_ref, qo_ref, ko_ref):
    cos = cos_ref[...]
    sin = sin_ref[...]
    bd = bd_ref[...]
    lane = lax.broadcasted_iota(jnp.int32, cos.shape, 1)
    first_half = (lane % DA_HEAD_DIM) < (DA_HEAD_DIM // 2)
    half = DA_HEAD_DIM // 2

    def prep(x_ref, g_ref, o_ref, scale):
        x = x_ref[...].astype(F32)
        hi, lo = _split_hi_lo(x * x)
        ms = _dot(hi, bd) + _dot(lo, bd)
        y = x * lax.rsqrt(ms + EPS) * g_ref[...]
        for c in range(DA_WIDTH // LANES):
            yc = y[:, c * LANES:(c + 1) * LANES]
            swapped = jnp.where(first_half, pltpu.roll(yc, LANES - half, 1), pltpu.roll(yc, half, 1))
            o_ref[:, c * LANES:(c + 1) * LANES] = ((yc * cos + swapped * sin) * scale).astype(BF16)

    prep(q_ref, qg_ref, qo_ref, DA_HEAD_DIM ** -0.5)
    prep(k_ref, kg_ref, ko_ref, 1.0)


def _qk_prep(proj, cos, sin, bd, qg, kg):
    T = proj.shape[0]
    tm = 1024
    W = DA_WIDTH
    return pl.pallas_call(
        _qk_prep_kernel,
        grid=(T // tm,),
        in_specs=[pl.BlockSpec((tm, W), lambda i: (i, COL_QA // W)),
                  pl.BlockSpec((tm, W), lambda i: (i, COL_KA // W)),
                  pl.BlockSpec((tm, LANES), lambda i: (i, 0)),
                  pl.BlockSpec((tm, LANES), lambda i: (i, 0)),
                  pl.BlockSpec((W, W), lambda i: (0, 0)),
                  pl.BlockSpec((1, W), lambda i: (0, 0)),
                  pl.BlockSpec((1, W), lambda i: (0, 0))],
        out_specs=[pl.BlockSpec((tm, W), lambda i: (i, 0))] * 2,
        out_shape=[jax.ShapeDtypeStruct((T, W), BF16)] * 2,
        compiler_params=_cparams(("arbitrary",)),
        name="qk_prep",
    )(proj, proj, cos, sin, bd, qg, kg)


def _diff_attn_kernel(shift_ref, lam_ref, q_ref, k_ref, v_ref, g_ref, o_ref, acc1, acc2, *, tq, out_scale, online):
    qi = pl.program_id(2)
    qf = q_ref[...].astype(F32)
    lane = lax.broadcasted_iota(jnp.int32, qf.shape, 1)
    qs = (jnp.where(lane < DA_HEAD_DIM, qf, 0.0).astype(BF16),
          jnp.where(lane >= DA_HEAD_DIM, qf, 0.0).astype(BF16))
    accs = (acc1, acc2)
    acc1[...] = jnp.zeros_like(acc1)
    acc2[...] = jnp.zeros_like(acc2)
    row = lax.broadcasted_iota(jnp.int32, (tq, tq), 0)
    col = lax.broadcasted_iota(jnp.int32, (tq, tq), 1)
    causal = col <= row
    shift = shift_ref[...]

    def step(kj, carry, masked):
        start = pl.multiple_of(kj * tq, tq)
        k = k_ref[pl.ds(start, tq), :]
        v = v_ref[pl.ds(start, tq), :]
        ss = [_dot_nt(q, k) for q in qs]
        if masked:
            ss = [jnp.where(causal, s, NEG_INF) for s in ss]
        if online:
            ms, ls = carry
            m_new = [jnp.maximum(m, jnp.max(s, axis=1, keepdims=True)) for m, s in zip(ms, ss)]
            ps = [jnp.exp(s - m) for s, m in zip(ss, m_new)]
            alphas = [jnp.exp(m - mn) for m, mn in zip(ms, m_new)]
            for acc, alpha, p in zip(accs, alphas, ps):
                acc[...] = alpha * acc[...] + _dot(p.astype(BF16), v)
            ls = [alpha * l + jnp.sum(p, axis=1, keepdims=True) for alpha, l, p in zip(alphas, ls, ps)]
            return tuple(m_new), tuple(ls)
        ps = [jnp.exp(s - shift) for s in ss]
        for acc, p in zip(accs, ps):
            acc[...] += _dot(p.astype(BF16), v)
        return tuple(l + jnp.sum(p, axis=1, keepdims=True) for l, p in zip(carry, ps))

    zero = jnp.zeros((tq, 1), F32)
    if online:
        neg = jnp.full((tq, 1), NEG_INF, F32)
        init = ((neg, neg), (zero, zero))
    else:
        init = (zero, zero)
    if online:
        carry = lax.fori_loop(0, qi, lambda kj, c: step(kj, c, False), init)
    else:
        def pair_step(t, ls):
            starts = [pl.multiple_of((2 * t + d) * tq, tq) for d in range(2)]
            ks = [k_ref[pl.ds(st, tq), :] for st in starts]
            vs = [v_ref[pl.ds(st, tq), :] for st in starts]
            ps = [[jnp.exp(_dot_nt(q, k) - shift) for q in qs] for k in ks]
            for blk, v in zip(ps, vs):
                for acc, p in zip(accs, blk):
                    acc[...] += _dot(p.astype(BF16), v)
            return tuple(l + jnp.sum(ps[0][m], axis=1, keepdims=True) + jnp.sum(ps[1][m], axis=1, keepdims=True)
                         for m, l in enumerate(ls))

        carry = lax.fori_loop(0, qi // 2, pair_step, init)
        carry = lax.cond(qi % 2 == 1, lambda c: step(qi - 1, c, False), lambda c: c, carry)
    carry = step(qi, carry, True)
    l1, l2 = carry[1] if online else carry
    o = acc1[...] / l1 - lam_ref[...] * (acc2[...] / l2)
    ms = jnp.mean(o * o, axis=-1, keepdims=True)
    o_ref[...] = (o * lax.rsqrt(ms + EPS) * g_ref[...] * out_scale).astype(BF16)


def _diff_attn(shift, lam_lanes, q_r, k_r, proj, subln_g, B, S, out_scale, online):
    T = q_r.shape[0]
    tq = 512
    nq = S // tq
    return pl.pallas_call(
        functools.partial(_diff_attn_kernel, tq=tq, out_scale=out_scale, online=online),
        grid=(B, DA_HEADS, nq),
        in_specs=[pl.BlockSpec((1, 1), lambda b, h, i: (0, 0)),
                  pl.BlockSpec((1, LANES), lambda b, h, i: (0, 0)),
                  pl.BlockSpec((tq, LANES), lambda b, h, i: (b * nq + i, h)),
                  pl.BlockSpec((S, LANES), lambda b, h, i: (b, h)),
                  pl.BlockSpec((S, LANES), lambda b, h, i: (b, COL_VA // LANES + h)),
                  pl.BlockSpec((1, LANES), lambda b, h, i: (0, 0))],
        out_specs=pl.BlockSpec((tq, LANES), lambda b, h, i: (b * nq + i, h)),
        out_shape=jax.ShapeDtypeStruct((T, DA_WIDTH), BF16),
        scratch_shapes=[pltpu.VMEM((tq, LANES), F32), pltpu.VMEM((tq, LANES), F32)],
        compiler_params=_cparams(("arbitrary", "arbitrary", "arbitrary")),
        name="diff_attn_online" if online else "diff_attn",
    )(shift, lam_lanes, q_r, k_r, proj, subln_g)


def _sb_attn_kernel(q_ref, k_ref, v_ref, u_ref, o_ref, acc, *, tq):
    qi = pl.program_id(2)
    qf = q_ref[...].astype(F32) * (SB_HEAD_DIM ** -0.5 * LOG2E)
    lane = lax.broadcasted_iota(jnp.int32, qf.shape, 1)
    qs = (jnp.where(lane < SB_HEAD_DIM, qf, 0.0).astype(BF16),
          jnp.where(lane >= SB_HEAD_DIM, qf, 0.0).astype(BF16))
    uu = u_ref[...]
    acc[...] = jnp.zeros_like(acc)
    row = lax.broadcasted_iota(jnp.int32, (tq, tq), 0)
    col = lax.broadcasted_iota(jnp.int32, (tq, tq), 1)
    before = col < row

    def step(kjs, rs, masked):
        ks = [k_ref[pl.ds(pl.multiple_of(kj * tq, tq), tq), :] for kj in kjs]
        vs = [v_ref[pl.ds(pl.multiple_of(kj * tq, tq), tq), :] for kj in kjs]
        chains = [(h, b) for b in range(len(kjs)) for h in range(2)]
        zs = [_dot_nt(qs[h], ks[b]) for h, b in chains]
        ps, hls = [], []
        for z in zs:
            neg_abs = pltpu.bitcast(pltpu.bitcast(z, jnp.uint32) | jnp.uint32(0x80000000), F32)
            p = jnp.maximum(z, 0.0) + jnp.log2(1.0 + jnp.exp2(neg_abs))
            if masked:
                p = jnp.where(before, p, 0.0)
            hi, lo = _split_hi_lo(p)
            ps.append(jnp.sum(p, axis=1, keepdims=True))
            hls.append(jnp.concatenate([hi, lo], axis=1))
        mms = [_dot(hl, uu) for hl in hls]
        rs = list(rs)
        for c, (h, b) in enumerate(chains):
            a = jnp.exp2(zs[c] + mms[c] + rs[h])
            if masked:
                a = jnp.where(before, a, 0.0)
            acc[h * tq:(h + 1) * tq, :] += _dot(a.astype(BF16), vs[b])
            rs[h] = rs[h] - ps[c]
        return tuple(rs)

    zero = jnp.zeros((tq, 1), F32)
    rs = step([qi], (zero, zero), True)
    def run(first, count, c):
        return step([first - d for d in range(count)], c, False)

    nb = SB_BLOCKS_PER_TRIP
    rs = lax.fori_loop(0, qi // nb, lambda t, c: run(qi - 1 - nb * t, nb, c), rs)
    left = qi % nb
    part = nb // 2
    while part >= 1:
        rs = lax.cond(left & part != 0, functools.partial(run, (left & (2 * part - 1)) - 1, part),
                      lambda c: c, rs)
        part //= 2

    o_ref[...] = jnp.where(lane < SB_HEAD_DIM, acc[0:tq, :], acc[tq:2 * tq, :]).astype(BF16)


def _sb_attn(proj, u, B, S):
    T = proj.shape[0]
    tq = SB_BLOCK
    nq = S // tq
    return pl.pallas_call(
        functools.partial(_sb_attn_kernel, tq=tq),
        grid=(B, SB_WIDTH // LANES, nq),
        in_specs=[pl.BlockSpec((tq, LANES), lambda b, p, i: (b * nq + i, COL_QC // LANES + p)),
                  pl.BlockSpec((S, LANES), lambda b, p, i: (b, COL_KC // LANES + p)),
                  pl.BlockSpec((S, LANES), lambda b, p, i: (b, COL_VC // LANES + p)),
                  pl.BlockSpec((2 * tq, tq), lambda b, p, i: (0, 0))],
        out_specs=pl.BlockSpec((tq, LANES), lambda b, p, i: (b * nq + i, p)),
        out_shape=jax.ShapeDtypeStruct((T, SB_WIDTH), BF16),
        scratch_shapes=[pltpu.VMEM((2 * tq, LANES), F32)],
        compiler_params=_cparams(("arbitrary", "arbitrary", "arbitrary")),
        name="sb_attn",
    )(proj, proj, proj, u)


def _conv_kernel(a_ref, g_ref, ah_ref, gh_ref, w_ref, b_ref, lg_ref, lb_ref, o_ref, hbuf, *, tm, per_batch, rows):
    i = pl.program_id(0)
    halo = ah_ref[...].astype(F32) * jax.nn.sigmoid(gh_ref[...].astype(F32))
    hbuf[0:CONV_HALO, :] = jnp.where(i % per_batch == 0, 0.0, halo)
    hbuf[CONV_HALO:, :] = a_ref[...].astype(F32) * jax.nn.sigmoid(g_ref[...].astype(F32))
    w = w_ref[...]
    for r0 in range(0, tm, rows):
        acc = jnp.broadcast_to(b_ref[...], (rows, CONV_CH))
        for b in range(SUBLANES):
            part = None
            for a in range((CONV_WIDTH - 1 - b) // SUBLANES + 1):
                lo = r0 + CONV_HALO - SUBLANES - SUBLANES * a
                tap = CONV_WIDTH - 1 - SUBLANES * a - b
                term = hbuf[lo:lo + rows + SUBLANES, :] * w[tap:tap + 1, :]
                part = term if part is None else part + term
            acc = acc + part[SUBLANES - b:SUBLANES - b + rows, :]
        mu = jnp.mean(acc, axis=-1, keepdims=True)
        d = acc - mu
        var = jnp.mean(d * d, axis=-1, keepdims=True)
        y = d * lax.rsqrt(var + EPS) * lg_ref[...] + lb_ref[...]
        o_ref[r0:r0 + rows, :] = (y * jax.nn.sigmoid(y)).astype(BF16)


def _conv_module(proj, w_dw, b_dw, ln_g, ln_b, S):
    T = proj.shape[0]
    tm = 512
    per_batch = S // tm
    C = CONV_CH
    ca, cg = COL_UB // C, COL_UB // C + 1
    halo_blocks = tm // CONV_HALO

    def halo_idx(col):
        return lambda i: (jnp.maximum(i * halo_blocks - 1, 0), col)

    return pl.pallas_call(
        functools.partial(_conv_kernel, tm=tm, per_batch=per_batch, rows=128),
        grid=(T // tm,),
        in_specs=[pl.BlockSpec((tm, C), lambda i: (i, ca)),
                  pl.BlockSpec((tm, C), lambda i: (i, cg)),
                  pl.BlockSpec((CONV_HALO, C), halo_idx(ca)),
                  pl.BlockSpec((CONV_HALO, C), halo_idx(cg)),
                  pl.BlockSpec((CONV_WIDTH, C), lambda i: (0, 0)),
                  pl.BlockSpec((1, C), lambda i: (0, 0)),
                  pl.BlockSpec((1, C), lambda i: (0, 0)),
                  pl.BlockSpec((1, C), lambda i: (0, 0))],
        out_specs=pl.BlockSpec((tm, C), lambda i: (i, 0)),
        out_shape=jax.ShapeDtypeStruct((T, C), BF16),
        scratch_shapes=[pltpu.VMEM((tm + CONV_HALO, C), F32)],
        compiler_params=_cparams(("arbitrary",)),
        name="conv_module",
    )(proj, proj, proj, proj, w_dw, b_dw, ln_g, ln_b)


def _merge_kernel(oa_ref, cb_ref, oc_ref, g0_ref, g1_ref, g2_ref, x_ref, gm_ref,
                  wa_ref, wb_ref, bb_ref, wc_ref, wo_ref, o_ref):
    y_a = _dot(oa_ref[...], wa_ref[...])
    y_b = _dot(cb_ref[...], wb_ref[...]) + bb_ref[...]
    y_c = _dot(oc_ref[...], wc_ref[...])
    merged = (jax.nn.sigmoid(g0_ref[...].astype(F32)) * y_a
              + jax.nn.sigmoid(g1_ref[...].astype(F32)) * y_b
              + jax.nn.sigmoid(g2_ref[...].astype(F32)) * y_c)
    o_ref[...] = x_ref[...] + gm_ref[0] * _dot(merged.astype(BF16), wo_ref[...])


def _merge(o_a, cb, o_c, proj, x, g_m, wa, wb, bb, wc, wo, S):
    T, D = x.shape
    tm = 512
    per_batch = S // tm
    W = DA_WIDTH
    gcol = COL_GATES // D
    branch = pl.BlockSpec((tm, W), lambda i: (i, 0))
    wspec = pl.BlockSpec((W, D), lambda i: (0, 0))
    return pl.pallas_call(
        _merge_kernel,
        grid=(T // tm,),
        in_specs=[branch, branch, branch,
                  pl.BlockSpec((tm, D), lambda i: (i, gcol)),
                  pl.BlockSpec((tm, D), lambda i: (i, gcol + 1)),
                  pl.BlockSpec((tm, D), lambda i: (i, gcol + 2)),
                  pl.BlockSpec((tm, D), lambda i: (i, 0)),
                  pl.BlockSpec((1, 1, D), lambda i: (i // per_batch, 0, 0)),
                  wspec, wspec, pl.BlockSpec((1, D), lambda i: (0, 0)), wspec,
                  pl.BlockSpec((D, D), lambda i: (0, 0))],
        out_specs=pl.BlockSpec((tm, D), lambda i: (i, 0)),
        out_shape=jax.ShapeDtypeStruct((T, D), F32),
        compiler_params=_cparams(("arbitrary",)),
        name="merge",
    )(o_a, cb, o_c, proj, proj, proj, x, g_m, wa, wb, bb, wc, wo)


def _first_index_of_max(vals, idx, size):
    mx = jnp.max(vals, axis=0, keepdims=True)
    first = jnp.min(jnp.where(vals == mx, idx, size), axis=0, keepdims=True)
    return mx, first


def _router_kernel(x_ref, g_ref, sc_ref, sh_ref, wh_ref, wl_ref, b_ref, before_ref, lower_ref,
                   h_ref, slot_ref, gate_ref, count_ref):
    h = _ada_rms(x_ref[...], g_ref[...], sc_ref[0], sh_ref[0])
    h_ref[...] = h.astype(BF16)
    h_hi, h_lo = _split_hi_lo(h)
    logits = _dot_nt(wh_ref[...], h_hi) + _dot_nt(wh_ref[...], h_lo) + _dot_nt(wl_ref[...], h_hi)
    scores = jax.nn.sigmoid(logits)
    biased = scores + b_ref[...]
    tm = scores.shape[1]
    minus_inf = -jnp.inf

    in_group = lax.broadcasted_iota(jnp.int32, (GROUP_SIZE, tm), 0)
    group_scores = []
    for g in range(N_GROUPS):
        vals = biased[g * GROUP_SIZE:(g + 1) * GROUP_SIZE]
        m1, i1 = _first_index_of_max(vals, in_group, GROUP_SIZE)
        m2 = jnp.max(jnp.where(in_group == i1, minus_inf, vals), axis=0, keepdims=True)
        group_scores.append(m1 + m2)
    gs = jnp.concatenate(group_scores, axis=0)

    gidx = lax.broadcasted_iota(jnp.int32, (N_GROUPS, tm), 0)
    group_sel = jnp.zeros((N_GROUPS, tm), jnp.bool_)
    for _ in range(TOPK_GROUPS):
        _, first = _first_index_of_max(gs, gidx, N_GROUPS)
        pick = gidx == first
        group_sel = jnp.logical_or(group_sel, pick)
        gs = jnp.where(pick, minus_inf, gs)

    eidx = lax.broadcasted_iota(jnp.int32, (N_EXPERTS, tm), 0)
    expert_group_sel = jnp.concatenate(
        [jnp.broadcast_to(group_sel[g:g + 1], (GROUP_SIZE, tm)) for g in range(N_GROUPS)], axis=0)
    cand = jnp.where(expert_group_sel, biased, NEG_INF)
    picks = []
    for _ in range(TOP_K):
        _, first = _first_index_of_max(cand, eidx, N_EXPERTS)
        pick = eidx == first
        picks.append(pick)
        cand = jnp.where(pick, minus_inf, cand)

    chosen = jnp.zeros((N_EXPERTS, tm), F32)
    for pick in picks:
        chosen = jnp.where(pick, 1.0, chosen)
    rank = _dot(chosen.astype(BF16), before_ref[...])
    count = jnp.sum(chosen, axis=1, keepdims=True).astype(jnp.int32)
    padded = (count + (SEG_ALIGN - 1)) & (-SEG_ALIGN)
    seg_start = _dot(lower_ref[...], jnp.broadcast_to(padded, (N_EXPERTS, LANES)).astype(BF16))[:, 0:1]
    slot_of = seg_start + rank
    slots = [jnp.sum(jnp.where(pick, slot_of, 0.0), axis=0, keepdims=True) for pick in picks]
    ws = [jnp.sum(jnp.where(pick, scores, 0.0), axis=0, keepdims=True) for pick in picks]
    norm = ROUTED_SCALE / (sum(ws) + 1e-20)
    slot_ref[...] = jnp.concatenate(slots, axis=0).astype(jnp.int32)
    gate_ref[...] = jnp.concatenate(ws, axis=0) * norm
    count_ref[0] = jnp.broadcast_to(count, (N_EXPERTS, LANES))


def _router(x1, g, sc, sh, wr_hi, wr_lo, b_router, S):
    T, D = x1.shape
    tm = MOE_TILE
    per_batch = S // tm
    E = N_EXPERTS
    tok = jnp.arange(tm, dtype=jnp.int32)
    before = (tok[:, None] < tok[None, :]).astype(BF16)
    ex = jnp.arange(E, dtype=jnp.int32)
    lower = (ex[None, :] < ex[:, None]).astype(BF16)
    return pl.pallas_call(
        _router_kernel,
        grid=(T // tm,),
        in_specs=[pl.BlockSpec((tm, D), lambda i: (i, 0)),
                  pl.BlockSpec((1, D), lambda i: (0, 0)),
                  pl.BlockSpec((1, 1, D), lambda i: (i // per_batch, 0, 0)),
                  pl.BlockSpec((1, 1, D), lambda i: (i // per_batch, 0, 0)),
                  pl.BlockSpec((E, D), lambda i: (0, 0)),
                  pl.BlockSpec((E, D), lambda i: (0, 0)),
                  pl.BlockSpec((E, 1), lambda i: (0, 0)),
                  pl.BlockSpec((tm, tm), lambda i: (0, 0)),
                  pl.BlockSpec((E, E), lambda i: (0, 0))],
        out_specs=[pl.BlockSpec((tm, D), lambda i: (i, 0)),
                   pl.BlockSpec((TOP_K, tm), lambda i: (0, i)),
                   pl.BlockSpec((TOP_K, tm), lambda i: (0, i)),
                   pl.BlockSpec((1, E, LANES), lambda i: (i, 0, 0))],
        out_shape=[jax.ShapeDtypeStruct((T, D), BF16),
                   jax.ShapeDtypeStruct((TOP_K, T), jnp.int32),
                   jax.ShapeDtypeStruct((TOP_K, T), F32),
                   jax.ShapeDtypeStruct((T // tm, E, LANES), jnp.int32)],
        compiler_params=_cparams(("arbitrary",)),
        name="router",
    )(x1, g, sc, sh, wr_hi, wr_lo, b_router, before, lower)


def _segment_pieces(tile, seg_start_ref, seg_dst_ref, seg_rows_ref, fn):
    def body(e, carry):
        idx = tile * N_EXPERTS + e
        local, dst, rows = seg_start_ref[idx], seg_dst_ref[idx], seg_rows_ref[idx]

        def pieces(sizes):
            for size in sizes:
                @pl.when(rows & size != 0)
                def _():
                    done = rows & (-2 * size)
                    fn(pl.multiple_of(local + done, SEG_ALIGN), pl.multiple_of(dst + done, SEG_ALIGN), size)

        pieces([s for s in SEG_PIECES if s < SEG_RARE])
        pl.when(rows >= SEG_RARE)(lambda: pieces([s for s in SEG_PIECES if s >= SEG_RARE]))
        return carry

    lax.fori_loop(0, N_EXPERTS, body, 0)


def _wait_pieces(tile, piece_count_ref, wait_one):
    for j, size in enumerate(SEG_PIECES):
        def body(_, carry, size=size):
            wait_one(size)
            return carry

        lax.fori_loop(0, piece_count_ref[tile * len(SEG_PIECES) + j], body, 0)


def _dispatch_kernel(seg_start_ref, seg_dst_ref, seg_rows_ref, piece_count_ref, h_ref, slot_ref, xs_in_ref, xs_ref,
                     xloc, sem, *, chunk):
    del xs_in_ref
    tile = pl.program_id(0)
    buf = tile % 2
    slots = slot_ref[...]
    h = h_ref[...]
    tm = h.shape[0]
    for r0 in range(0, xloc.shape[1], chunk):
        s = lax.broadcasted_iota(jnp.int32, (chunk, tm), 0) + r0
        onehot = jnp.zeros((chunk, tm), F32)
        for k in range(TOP_K):
            onehot = jnp.where(s == slots[k:k + 1, :], 1.0, onehot)
        xloc[buf, r0:r0 + chunk, :] = _dot(onehot.astype(BF16), h).astype(BF16)

    def copy(b, local, dst, rows):
        return pltpu.make_async_copy(xloc.at[b, pl.ds(local, rows)], xs_ref.at[pl.ds(dst, rows)], sem.at[b])

    _segment_pieces(tile, seg_start_ref, seg_dst_ref, seg_rows_ref, lambda *a: copy(buf, *a).start())

    @pl.when(tile > 0)
    def _():
        _wait_pieces(tile - 1, piece_count_ref, lambda rows: copy(1 - buf, 0, 0, rows).wait())

    @pl.when(tile == pl.num_programs(0) - 1)
    def _():
        _wait_pieces(tile, piece_count_ref, lambda rows: copy(buf, 0, 0, rows).wait())


def _dispatch(seg_start, seg_dst, seg_rows, piece_count, h2, slots, xs_buffer):
    T, D = h2.shape
    tm = MOE_TILE
    return pl.pallas_call(
        functools.partial(_dispatch_kernel, chunk=512),
        grid_spec=pltpu.PrefetchScalarGridSpec(
            num_scalar_prefetch=4,
            grid=(T // tm,),
            in_specs=[pl.BlockSpec((tm, D), lambda i, *_: (i, 0)),
                      pl.BlockSpec((TOP_K, tm), lambda i, *_: (0, i)),
                      pl.BlockSpec(memory_space=pl.ANY)],
            out_specs=pl.BlockSpec(memory_space=pl.ANY),
            scratch_shapes=[pltpu.VMEM((2, LOCAL_SLOTS, D), BF16), pltpu.SemaphoreType.DMA((2,))]),
        out_shape=jax.ShapeDtypeStruct(xs_buffer.shape, BF16),
        input_output_aliases={6: 0},
        compiler_params=_cparams(("arbitrary",)),
        name="moe_dispatch",
    )(seg_start, seg_dst, seg_rows, piece_count, h2, slots, xs_buffer)


def _expert_kernel(blk_expert_ref, blk_rows_ref, x_ref, w1_ref, w3_ref, w2_ref, y_ref):
    del blk_expert_ref
    rows = blk_rows_ref[pl.program_id(0)]

    @pl.when(rows > 0)
    def _():
        x = x_ref[...]
        a = _dot(x, w1_ref[0].astype(BF16))
        hid = a * jax.nn.sigmoid(a) * _dot(x, w3_ref[0].astype(BF16))
        y = _dot(hid.astype(BF16), w2_ref[0].astype(BF16))
        r = lax.broadcasted_iota(jnp.int32, y.shape, 0)
        y_ref[...] = jnp.where(r < rows, y, 0.0).astype(BF16)

    @pl.when(rows == 0)
    def _():
        y_ref[...] = jnp.zeros_like(y_ref)


def _experts(blk_expert, blk_rows, xs, w1, w3, w2, layer):
    n_rows, D = xs.shape
    F = EXPERT_FF
    R = EXPERT_BLOCK_ROWS
    return pl.pallas_call(
        _expert_kernel,
        grid_spec=pltpu.PrefetchScalarGridSpec(
            num_scalar_prefetch=2,
            grid=(n_rows // R,),
            in_specs=[pl.BlockSpec((R, D), lambda b, be, br: (b, 0)),
                      pl.BlockSpec((None, 1, D, F), lambda b, be, br: (layer, be[b], 0, 0)),
                      pl.BlockSpec((None, 1, D, F), lambda b, be, br: (layer, be[b], 0, 0)),
                      pl.BlockSpec((None, 1, F, D), lambda b, be, br: (layer, be[b], 0, 0))],
            out_specs=pl.BlockSpec((R, D), lambda b, be, br: (b, 0))),
        out_shape=jax.ShapeDtypeStruct((n_rows, D), BF16),
        compiler_params=_cparams(("arbitrary",)),
        name="moe_experts",
    )(blk_expert, blk_rows, xs, w1, w3, w2)


def _combine_kernel(seg_start_ref, seg_dst_ref, seg_rows_ref, piece_count_ref, ys_ref, slot_ref, gate_ref, h_ref,
                    x_ref, gf_ref, ws1_ref, ws3_ref, ws2_ref, o_ref, yloc, sem, *, chunk):
    tile = pl.program_id(0)
    buf = tile % 2

    def copy(b, local, dst, rows):
        return pltpu.make_async_copy(ys_ref.at[pl.ds(dst, rows)], yloc.at[b, pl.ds(local, rows)], sem.at[b])

    def fetch(t, b):
        yloc[b] = jnp.zeros(yloc.shape[1:], BF16)
        _segment_pieces(t, seg_start_ref, seg_dst_ref, seg_rows_ref, lambda *a: copy(b, *a).start())

    pl.when(tile == 0)(lambda: fetch(tile, buf))
    pl.when(tile + 1 < pl.num_programs(0))(lambda: fetch(tile + 1, 1 - buf))
    h = h_ref[...]
    a = _dot(h, ws1_ref[...])
    total = _dot((a * jax.nn.sigmoid(a) * _dot(h, ws3_ref[...])).astype(BF16), ws2_ref[...])
    _wait_pieces(tile, piece_count_ref, lambda rows: copy(buf, 0, 0, rows).wait())

    slots = slot_ref[...]
    gates = gate_ref[...]
    tm = h.shape[0]
    for c0 in range(0, yloc.shape[1], chunk):
        s = lax.broadcasted_iota(jnp.int32, (tm, chunk), 1) + c0
        weights = jnp.zeros((tm, chunk), F32)
        for k in range(TOP_K):
            weights = jnp.where(s == slots[:, k:k + 1], gates[:, k:k + 1], weights)
        total = total + _dot(weights.astype(BF16), yloc[buf, c0:c0 + chunk, :])
    o_ref[...] = x_ref[...] + gf_ref[0] * total


def _combine(seg_start, seg_dst, seg_rows, piece_count, ys, slots_tok, gates_tok, h2, x1, g_f, ws1, ws3, ws2, S):
    T, D = x1.shape
    tm = MOE_TILE
    per_batch = S // tm
    F = ws1.shape[1]
    return pl.pallas_call(
        functools.partial(_combine_kernel, chunk=512),
        grid_spec=pltpu.PrefetchScalarGridSpec(
            num_scalar_prefetch=4,
            grid=(T // tm,),
            in_specs=[pl.BlockSpec(memory_space=pl.ANY),
                      pl.BlockSpec((tm, TOP_K), lambda i, *_: (i, 0)),
                      pl.BlockSpec((tm, TOP_K), lambda i, *_: (i, 0)),
                      pl.BlockSpec((tm, D), lambda i, *_: (i, 0)),
                      pl.BlockSpec((tm, D), lambda i, *_: (i, 0)),
                      pl.BlockSpec((1, 1, D), lambda i, *_: (i // per_batch, 0, 0)),
                      pl.BlockSpec((D, F), lambda i, *_: (0, 0)),
                      pl.BlockSpec((D, F), lambda i, *_: (0, 0)),
                      pl.BlockSpec((F, D), lambda i, *_: (0, 0))],
            out_specs=pl.BlockSpec((tm, D), lambda i, *_: (i, 0)),
            scratch_shapes=[pltpu.VMEM((2, LOCAL_SLOTS, D), BF16), pltpu.SemaphoreType.DMA((2,))]),
        out_shape=jax.ShapeDtypeStruct((T, D), F32),
        compiler_params=_cparams(("arbitrary",)),
        name="moe_combine",
    )(seg_start, seg_dst, seg_rows, piece_count, ys, slots_tok, gates_tok, h2, x1, g_f, ws1, ws3, ws2)


def _moe_buffer_rows(T):
    R = EXPERT_BLOCK_ROWS
    n_segments = (T // MOE_TILE) * N_EXPERTS
    return -(-(T * TOP_K + n_segments * (SEG_ALIGN - 1) + N_EXPERTS * (R - SEG_ALIGN)) // R) * R


def _moe(x1, norm_g, sc_f, sh_f, g_f, w_router, b_router, w1, w3, w2, layer, ws1, ws3, ws2, S, xs_buffer):
    T, D = x1.shape
    E = N_EXPERTS
    R = EXPERT_BLOCK_ROWS
    wr_t = w_router.T
    wr_hi = wr_t.astype(BF16)
    wr_lo = (wr_t - wr_hi.astype(F32)).astype(BF16)
    h2, slots, gates, counts = _router(x1, norm_g.reshape(1, D), sc_f, sh_f, wr_hi, wr_lo, b_router.reshape(E, 1), S)

    seg_rows = (counts[:, :, 0] + (SEG_ALIGN - 1)) // SEG_ALIGN * SEG_ALIGN
    seg_start = jnp.cumsum(seg_rows, axis=1) - seg_rows
    expert_rows = jnp.sum(seg_rows, axis=0)
    region = (expert_rows + (R - 1)) // R * R
    region_end = jnp.cumsum(region)
    region_start = region_end - region
    seg_dst = region_start[None, :] + jnp.cumsum(seg_rows, axis=0) - seg_rows
    blk_first = jnp.arange(xs_buffer.shape[0] // R, dtype=jnp.int32) * R
    blk_expert = jnp.minimum(jnp.sum(blk_first[:, None] >= region_end[None, :], axis=1), E - 1).astype(jnp.int32)
    blk_rows = jnp.clip(expert_rows[blk_expert] - (blk_first - region_start[blk_expert]), 0, R).astype(jnp.int32)

    piece_count = jnp.stack([jnp.sum((seg_rows & size) != 0, axis=1) for size in SEG_PIECES], axis=1)

    flat = lambda a: a.reshape(-1).astype(jnp.int32)
    tables = (flat(seg_start), flat(seg_dst), flat(seg_rows), flat(piece_count))
    xs = _dispatch(*tables, h2, slots, xs_buffer)
    ys = _experts(blk_expert, blk_rows, xs, w1, w3, w2, layer)
    out = _combine(*tables, ys, slots.T, gates.T, h2, x1, g_f,
                   ws1.astype(BF16), ws3.astype(BF16), ws2.astype(BF16), S)
    return out, xs


def kernel(x, c, positions, w_mod, b_mod, norm_mix_g, norm_ffn_g, w_in, qn_g, kn_g, lam_q1, lam_k1, lam_q2, lam_k2, subln_g, w_proj_a, w_dw, b_dw, conv_ln_g, conv_ln_b, w_proj_b, b_proj_b, w_proj_c, w_out, w_router, b_router, w1, w3, w2, ws1, ws3, ws2):
    B, S, D = x.shape
    T = B * S
    depth = w_mod.shape[0]
    xt = x.reshape(T, D)

    inv = ROPE_THETA ** (-jnp.arange(0, DA_HEAD_DIM, 2, dtype=F32) / DA_HEAD_DIM)
    inv_lanes = jnp.tile(inv, LANES // (DA_HEAD_DIM // 2)).reshape(1, LANES)
    pos_lanes = jnp.broadcast_to(positions.astype(F32).reshape(T, 1), (T, LANES))
    cos, sin = _rope_tables(pos_lanes, inv_lanes)

    mod = _modulation(c, w_mod, b_mod)

    seg = jnp.arange(DA_WIDTH, dtype=jnp.int32) // DA_HEAD_DIM
    bd = jnp.where(seg[:, None] == seg[None, :], 1.0 / DA_HEAD_DIM, 0.0).astype(BF16)
    kk = jnp.arange(SB_BLOCK, dtype=jnp.int32)
    u = jnp.where(kk[:, None] >= kk[None, :], -1.0, 0.0).astype(BF16)
    u = jnp.concatenate([u, u], axis=0)
    n_seg = DA_WIDTH // DA_HEAD_DIM

    xs_buffer = jnp.zeros((_moe_buffer_rows(T), D), BF16)
    for l in range(depth):
        lambda_init = 0.8 - 0.6 * math.exp(-0.3 * l)
        sh_m, sc_m, g_m, sh_f, sc_f, g_f = [m.reshape(B, 1, D) for m in jnp.split(mod[l], 6, axis=-1)]

        proj = _norm_proj(xt, norm_mix_g[l].reshape(1, D), sc_m, sh_m, w_in[l].astype(BF16), S)

        q_r, k_r = _qk_prep(proj, cos, sin, bd,
                            jnp.tile(qn_g[l], n_seg).reshape(1, DA_WIDTH),
                            jnp.tile(kn_g[l], n_seg).reshape(1, DA_WIDTH))
        lam = (jnp.exp(jnp.sum(lam_q1[l] * lam_k1[l])) - jnp.exp(jnp.sum(lam_q2[l] * lam_k2[l])) + lambda_init)
        score_bound = (DA_HEAD_DIM ** 0.5) * jnp.max(jnp.abs(qn_g[l])) * jnp.max(jnp.abs(kn_g[l]))
        attn = functools.partial(_diff_attn, B=B, S=S, out_scale=1.0 - lambda_init)
        o_a = lax.cond(score_bound <= MAX_FIXED_SHIFT,
                       functools.partial(attn, online=False), functools.partial(attn, online=True),
                       score_bound.reshape(1, 1).astype(F32), jnp.full((1, LANES), lam, F32), q_r, k_r, proj,
                       subln_g[l].reshape(1, LANES))
        cb = _conv_module(proj, w_dw[l], b_dw[l].reshape(1, CONV_CH), conv_ln_g[l].reshape(1, CONV_CH),
                          conv_ln_b[l].reshape(1, CONV_CH), S)
        o_c = _sb_attn(proj, u, B, S)
        x1 = _merge(o_a, cb, o_c, proj, xt, g_m, w_proj_a[l].astype(BF16), w_proj_b[l].astype(BF16),
                    b_proj_b[l].reshape(1, D), w_proj_c[l].astype(BF16), w_out[l].astype(BF16), S)

        xt, xs_buffer = _moe(x1, norm_ffn_g[l], sc_f, sh_f, g_f, w_router[l], b_router[l],
                             w1, w3, w2, l, ws1[l], ws3[l], ws2[l], S, xs_buffer)

    return xt.reshape(B, S, D)
```

```python
import functools
import math

import jax
import jax.numpy as jnp
from jax import lax
from jax.experimental import pallas as pl
from jax.experimental.pallas import tpu as pltpu

F32 = jnp.float32
BF16 = jnp.bfloat16

D_MODEL = 1024
DA_HEADS = 4
DA_HEAD_DIM = 64
DA_WIDTH = 512
CONV_CH = 512
CONV_WIDTH = 31
SB_WIDTH = 512
SB_HEAD_DIM = 64
IN_COLS = 7168
ROPE_THETA = 10000.0
EPS = 1e-6
NEG_INF = -1e30
LOG2E = math.log2(math.e)
MAX_FIXED_SHIFT = 40.0
SB_BLOCK = 256
SB_BLOCKS_PER_TRIP = 4
N_EXPERTS = 64
TOP_K = 8
N_GROUPS = 8
TOPK_GROUPS = 4
GROUP_SIZE = N_EXPERTS // N_GROUPS
EXPERT_FF = 256
ROUTED_SCALE = 2.5

MOE_TILE = 512
SEG_ALIGN = 16
SEG_PIECES = tuple(SEG_ALIGN << j for j in reversed(range((MOE_TILE // SEG_ALIGN).bit_length())))
LOCAL_SLOTS = -(-(MOE_TILE * TOP_K + N_EXPERTS * (SEG_ALIGN - 1)) // 512) * 512
SEG_RARE = 128
EXPERT_BLOCK_ROWS = 512
LANES = 128
SUBLANES = 8
CONV_HALO = 32

COL_QA, COL_KA, COL_VA, COL_UB, COL_QC, COL_KC, COL_VC, COL_GATES = 0, 512, 1024, 1536, 2560, 3072, 3584, 4096

VMEM_LIMIT = 48 * 1024 * 1024


def _cparams(sem):
    return pltpu.CompilerParams(dimension_semantics=sem, vmem_limit_bytes=VMEM_LIMIT)


def _dot(a, b):
    return jnp.dot(a, b, preferred_element_type=F32)


def _dot_nt(a, b):
    return lax.dot_general(a, b, (((1,), (1,)), ((), ())), preferred_element_type=F32)


def _split_hi_lo(x):
    hi = x.astype(BF16)
    lo = (x - hi.astype(F32)).astype(BF16)
    return hi, lo


def _mod_kernel(c_ref, w_ref, b_ref, o_ref):
    c = c_ref[...]
    c_act = c * jax.nn.sigmoid(c)
    o_ref[0] = jnp.dot(c_act, w_ref[0], precision=lax.Precision.HIGHEST,
                       preferred_element_type=F32) + b_ref[0]


def _modulation(c, w_mod, b_mod):
    L, D, N = w_mod.shape
    B = c.shape[0]
    tn = 1536
    return pl.pallas_call(
        _mod_kernel,
        grid=(L, N // tn),
        in_specs=[pl.BlockSpec((B, D), lambda l, j: (0, 0)),
                  pl.BlockSpec((1, D, tn), lambda l, j: (l, 0, j)),
                  pl.BlockSpec((1, 1, tn), lambda l, j: (l, 0, j))],
        out_specs=pl.BlockSpec((1, B, tn), lambda l, j: (l, 0, j)),
        out_shape=jax.ShapeDtypeStruct((L, B, N), F32),
        compiler_params=_cparams(("arbitrary", "arbitrary")),
        name="modulation",
    )(c, w_mod, b_mod.reshape(L, 1, N))


def _rope_kernel(pos_ref, inv_ref, cos_ref, sin_ref):
    ang = pos_ref[...] * inv_ref[...]
    lane = lax.broadcasted_iota(jnp.int32, ang.shape, 1)
    first_half = (lane % DA_HEAD_DIM) < (DA_HEAD_DIM // 2)
    cos_ref[...] = jnp.cos(ang)
    s = jnp.sin(ang)
    sin_ref[...] = jnp.where(first_half, -s, s)


def _rope_tables(pos_lanes, inv_lanes):
    T = pos_lanes.shape[0]
    tm = 2048
    return pl.pallas_call(
        _rope_kernel,
        grid=(T // tm,),
        in_specs=[pl.BlockSpec((tm, LANES), lambda i: (i, 0)),
                  pl.BlockSpec((1, LANES), lambda i: (0, 0))],
        out_specs=[pl.BlockSpec((tm, LANES), lambda i: (i, 0))] * 2,
        out_shape=[jax.ShapeDtypeStruct((T, LANES), F32)] * 2,
        compiler_params=_cparams(("arbitrary",)),
        name="rope_tables",
    )(pos_lanes, inv_lanes)


def _ada_rms(x, g, sc, sh):
    ms = jnp.mean(x * x, axis=-1, keepdims=True)
    return x * lax.rsqrt(ms + EPS) * g * (1.0 + sc) + sh


def _norm_proj_kernel(x_ref, g_ref, sc_ref, sh_ref, w_ref, o_ref, h_scr):
    @pl.when(pl.program_id(1) == 0)
    def _():
        h_scr[...] = _ada_rms(x_ref[...], g_ref[...], sc_ref[0], sh_ref[0]).astype(BF16)

    o_ref[...] = _dot(h_scr[...], w_ref[...]).astype(BF16)


def _norm_proj(x, g, sc, sh, w_bf16, S):
    T, D = x.shape
    N = w_bf16.shape[1]
    tm, tn = 2048, 1024
    per_batch = S // tm
    return pl.pallas_call(
        _norm_proj_kernel,
        grid=(T // tm, N // tn),
        in_specs=[pl.BlockSpec((tm, D), lambda i, j: (i, 0)),
                  pl.BlockSpec((1, D), lambda i, j: (0, 0)),
                  pl.BlockSpec((1, 1, D), lambda i, j: (i // per_batch, 0, 0)),
                  pl.BlockSpec((1, 1, D), lambda i, j: (i // per_batch, 0, 0)),
                  pl.BlockSpec((D, tn), lambda i, j: (0, j))],
        out_specs=pl.BlockSpec((tm, tn), lambda i, j: (i, j)),
        out_shape=jax.ShapeDtypeStruct((T, N), BF16),
        scratch_shapes=[pltpu.VMEM((tm, D), BF16)],
        compiler_params=_cparams(("arbitrary", "arbitrary")),
        name="norm_proj",
    )(x, g, sc, sh, w_bf16)


def _qk_prep_kernel(q_ref, k_ref, cos_ref, sin_ref, bd_ref, qg_ref, kg_ref, qo_ref, ko_ref):
    cos = cos_ref[...]
    sin = sin_ref[...]
    bd = bd_ref[...]
    lane = lax.broadcasted_iota(jnp.int32, cos.shape, 1)
    first_half = (lane % DA_HEAD_DIM) < (DA_HEAD_DIM // 2)
    half = DA_HEAD_DIM // 2

    def prep(x_ref, g_ref, o_ref, scale):
        x = x_ref[...].astype(F32)
        hi, lo = _split_hi_lo(x * x)
        ms = _dot(hi, bd) + _dot(lo, bd)
        y = x * lax.rsqrt(ms + EPS) * g_ref[...]
        for c in range(DA_WIDTH // LANES):
            yc = y[:, c * LANES:(c + 1) * LANES]
            swapped = jnp.where(first_half, pltpu.roll(yc, LANES - half, 1), pltpu.roll(yc, half, 1))
            o_ref[:, c * LANES:(c + 1) * LANES] = ((yc * cos + swapped * sin) * scale).astype(BF16)

    prep(q_ref, qg_ref, qo_ref, DA_HEAD_DIM ** -0.5)
    prep(k_ref, kg_ref, ko_ref, 1.0)


def _qk_prep(proj, cos, sin, bd, qg, kg):
    T = proj.shape[0]
    tm = 1024
    W = DA_WIDTH
    return pl.pallas_call(
        _qk_prep_kernel,
        grid=(T // tm,),
        in_specs=[pl.BlockSpec((tm, W), lambda i: (i, COL_QA // W)),
                  pl.BlockSpec((tm, W), lambda i: (i, COL_KA // W)),
                  pl.BlockSpec((tm, LANES), lambda i: (i, 0)),
                  pl.BlockSpec((tm, LANES), lambda i: (i, 0)),
                  pl.BlockSpec((W, W), lambda i: (0, 0)),
                  pl.BlockSpec((1, W), lambda i: (0, 0)),
                  pl.BlockSpec((1, W), lambda i: (0, 0))],
        out_specs=[pl.BlockSpec((tm, W), lambda i: (i, 0))] * 2,
        out_shape=[jax.ShapeDtypeStruct((T, W), BF16)] * 2,
        compiler_params=_cparams(("arbitrary",)),
        name="qk_prep",
    )(proj, proj, cos, sin, bd, qg, kg)


def _diff_attn_kernel(shift_ref, lam_ref, q_ref, k_ref, v_ref, g_ref, o_ref, acc1, acc2, *, tq, out_scale, online):
    qi = pl.program_id(2)
    qf = q_ref[...].astype(F32)
    lane = lax.broadcasted_iota(jnp.int32, qf.shape, 1)
    qs = (jnp.where(lane < DA_HEAD_DIM, qf, 0.0).astype(BF16),
          jnp.where(lane >= DA_HEAD_DIM, qf, 0.0).astype(BF16))
    accs = (acc1, acc2)
    acc1[...] = jnp.zeros_like(acc1)
    acc2[...] = jnp.zeros_like(acc2)
    row = lax.broadcasted_iota(jnp.int32, (tq, tq), 0)
    col = lax.broadcasted_iota(jnp.int32, (tq, tq), 1)
    causal = col <= row
    shift = shift_ref[...]

    def step(kj, carry, masked):
        start = pl.multiple_of(kj * tq, tq)
        k = k_ref[pl.ds(start, tq), :]
        v = v_ref[pl.ds(start, tq), :]
        ss = [_dot_nt(q, k) for q in qs]
        if masked:
            ss = [jnp.where(causal, s, NEG_INF) for s in ss]
        if online:
            ms, ls = carry
            m_new = [jnp.maximum(m, jnp.max(s, axis=1, keepdims=True)) for m, s in zip(ms, ss)]
            ps = [jnp.exp(s - m) for s, m in zip(ss, m_new)]
            alphas = [jnp.exp(m - mn) for m, mn in zip(ms, m_new)]
            for acc, alpha, p in zip(accs, alphas, ps):
                acc[...] = alpha * acc[...] + _dot(p.astype(BF16), v)
            ls = [alpha * l + jnp.sum(p, axis=1, keepdims=True) for alpha, l, p in zip(alphas, ls, ps)]
            return tuple(m_new), tuple(ls)
        ps = [jnp.exp(s - shift) for s in ss]
        for acc, p in zip(accs, ps):
            acc[...] += _dot(p.astype(BF16), v)
        return tuple(l + jnp.sum(p, axis=1, keepdims=True) for l, p in zip(carry, ps))

    zero = jnp.zeros((tq, 1), F32)
    if online:
        neg = jnp.full((tq, 1), NEG_INF, F32)
        init = ((neg, neg), (zero, zero))
    else:
        init = (zero, zero)
    if online:
        carry = lax.fori_loop(0, qi, lambda kj, c: step(kj, c, False), init)
    else:
        def pair_step(t, ls):
            starts = [pl.multiple_of((2 * t + d) * tq, tq) for d in range(2)]
            ks = [k_ref[pl.ds(st, tq), :] for st in starts]
            vs = [v_ref[pl.ds(st, tq), :] for st in starts]
            ps = [[jnp.exp(_dot_nt(q, k) - shift) for q in qs] for k in ks]
            for blk, v in zip(ps, vs):
                for acc, p in zip(accs, blk):
                    acc[...] += _dot(p.astype(BF16), v)
            return tuple(l + jnp.sum(ps[0][m], axis=1, keepdims=True) + jnp.sum(ps[1][m], axis=1, keepdims=True)
                         for m, l in enumerate(ls))

        carry = lax.fori_loop(0, qi // 2, pair_step, init)
        carry = lax.cond(qi % 2 == 1, lambda c: step(qi - 1, c, False), lambda c: c, carry)
    carry = step(qi, carry, True)
    l1, l2 = carry[1] if online else carry
    o = acc1[...] / l1 - lam_ref[...] * (acc2[...] / l2)
    ms = jnp.mean(o * o, axis=-1, keepdims=True)
    o_ref[...] = (o * lax.rsqrt(ms + EPS) * g_ref[...] * out_scale).astype(BF16)


def _diff_attn(shift, lam_lanes, q_r, k_r, proj, subln_g, B, S, out_scale, online):
    T = q_r.shape[0]
    tq = 512
    nq = S // tq
    return pl.pallas_call(
        functools.partial(_diff_attn_kernel, tq=tq, out_scale=out_scale, online=online),
        grid=(B, DA_HEADS, nq),
        in_specs=[pl.BlockSpec((1, 1), lambda b, h, i: (0, 0)),
                  pl.BlockSpec((1, LANES), lambda b, h, i: (0, 0)),
                  pl.BlockSpec((tq, LANES), lambda b, h, i: (b * nq + i, h)),
                  pl.BlockSpec((S, LANES), lambda b, h, i: (b, h)),
                  pl.BlockSpec((S, LANES), lambda b, h, i: (b, COL_VA // LANES + h)),
                  pl.BlockSpec((1, LANES), lambda b, h, i: (0, 0))],
        out_specs=pl.BlockSpec((tq, LANES), lambda b, h, i: (b * nq + i, h)),
        out_shape=jax.ShapeDtypeStruct((T, DA_WIDTH), BF16),
        scratch_shapes=[pltpu.VMEM((tq, LANES), F32), pltpu.VMEM((tq, LANES), F32)],
        compiler_params=_cparams(("arbitrary", "arbitrary", "arbitrary")),
        name="diff_attn_online" if online else "diff_attn",
    )(shift, lam_lanes, q_r, k_r, proj, subln_g)


def _sb_attn_kernel(q_ref, k_ref, v_ref, u_ref, o_ref, acc, *, tq):
    qi = pl.program_id(2)
    qf = q_ref[...].astype(F32) * (SB_HEAD_DIM ** -0.5 * LOG2E)
    lane = lax.broadcasted_iota(jnp.int32, qf.shape, 1)
    qs = (jnp.where(lane < SB_HEAD_DIM, qf, 0.0).astype(BF16),
          jnp.where(lane >= SB_HEAD_DIM, qf, 0.0).astype(BF16))
    uu = u_ref[...]
    acc[...] = jnp.zeros_like(acc)
    row = lax.broadcasted_iota(jnp.int32, (tq, tq), 0)
    col = lax.broadcasted_iota(jnp.int32, (tq, tq), 1)
    before = col < row

    def step(kjs, rs, masks):
        ks = [k_ref[pl.ds(pl.multiple_of(kj * tq, tq), tq), :] for kj in kjs]
        vs = [v_ref[pl.ds(pl.multiple_of(kj * tq, tq), tq), :] for kj in kjs]
        chains = [(h, b) for b in range(len(kjs)) for h in range(2)]
        zs = [_dot_nt(qs[h], ks[b]) for h, b in chains]
        ps, hls = [], []
        for z, (_, b) in zip(zs, chains):
            neg_abs = pltpu.bitcast(pltpu.bitcast(z, jnp.uint32) | jnp.uint32(0x80000000), F32)
            p = jnp.maximum(z, 0.0) + jnp.log2(1.0 + jnp.exp2(neg_abs))
            if masks[b]:
                p = jnp.where(before, p, 0.0)
            hi, lo = _split_hi_lo(p)
            ps.append(jnp.sum(p, axis=1, keepdims=True))
            hls.append(jnp.concatenate([hi, lo], axis=1))
        mms = [_dot(hl, uu) for hl in hls]
        rs = list(rs)
        for c, (h, b) in enumerate(chains):
            a = jnp.exp2(zs[c] + mms[c] + rs[h])
            if masks[b]:
                a = jnp.where(before, a, 0.0)
            acc[h * tq:(h + 1) * tq, :] += _dot(a.astype(BF16), vs[b])
            rs[h] = rs[h] - ps[c]
        return tuple(rs)

    zero = jnp.zeros((tq, 1), F32)
    rs = lax.cond(qi % 2 == 1,
                  lambda c: step([qi, qi - 1], c, (True, False)),
                  lambda c: step([qi], c, (True,)), (zero, zero))
    full = qi - qi % 2

    def run(first, count, c):
        return step([first - d for d in range(count)], c, (False,) * count)

    nb = SB_BLOCKS_PER_TRIP
    rs = lax.fori_loop(0, full // nb, lambda t, c: run(full - 1 - nb * t, nb, c), rs)
    left = full % nb
    part = nb // 2
    while part >= 2:
        rs = lax.cond(left & part != 0, functools.partial(run, (left & (2 * part - 1)) - 1, part),
                      lambda c: c, rs)
        part //= 2

    o_ref[...] = jnp.where(lane < SB_HEAD_DIM, acc[0:tq, :], acc[tq:2 * tq, :]).astype(BF16)


def _sb_attn(proj, u, B, S):
    T = proj.shape[0]
    tq = SB_BLOCK
    nq = S // tq
    return pl.pallas_call(
        functools.partial(_sb_attn_kernel, tq=tq),
        grid=(B, SB_WIDTH // LANES, nq),
        in_specs=[pl.BlockSpec((tq, LANES), lambda b, p, i: (b * nq + i, COL_QC // LANES + p)),
                  pl.BlockSpec((S, LANES), lambda b, p, i: (b, COL_KC // LANES + p)),
                  pl.BlockSpec((S, LANES), lambda b, p, i: (b, COL_VC // LANES + p)),
                  pl.BlockSpec((2 * tq, tq), lambda b, p, i: (0, 0))],
        out_specs=pl.BlockSpec((tq, LANES), lambda b, p, i: (b * nq + i, p)),
        out_shape=jax.ShapeDtypeStruct((T, SB_WIDTH), BF16),
        scratch_shapes=[pltpu.VMEM((2 * tq, LANES), F32)],
        compiler_params=_cparams(("arbitrary", "arbitrary", "arbitrary")),
        name="sb_attn",
    )(proj, proj, proj, u)


def _conv_kernel(a_ref, g_ref, ah_ref, gh_ref, w_ref, b_ref, lg_ref, lb_ref, o_ref, hbuf, *, tm, per_batch, rows):
    i = pl.program_id(0)
    halo = ah_ref[...].astype(F32) * jax.nn.sigmoid(gh_ref[...].astype(F32))
    hbuf[0:CONV_HALO, :] = jnp.where(i % per_batch == 0, 0.0, halo)
    hbuf[CONV_HALO:, :] = a_ref[...].astype(F32) * jax.nn.sigmoid(g_ref[...].astype(F32))
    w = w_ref[...]
    for r0 in range(0, tm, rows):
        acc = jnp.broadcast_to(b_ref[...], (rows, CONV_CH))
        for b in range(SUBLANES):
            part = None
            for a in range((CONV_WIDTH - 1 - b) // SUBLANES + 1):
                lo = r0 + CONV_HALO - SUBLANES - SUBLANES * a
                tap = CONV_WIDTH - 1 - SUBLANES * a - b
                term = hbuf[lo:lo + rows + SUBLANES, :] * w[tap:tap + 1, :]
                part = term if part is None else part + term
            acc = acc + part[SUBLANES - b:SUBLANES - b + rows, :]
        mu = jnp.mean(acc, axis=-1, keepdims=True)
        d = acc - mu
        var = jnp.mean(d * d, axis=-1, keepdims=True)
        y = d * lax.rsqrt(var + EPS) * lg_ref[...] + lb_ref[...]
        o_ref[r0:r0 + rows, :] = (y * jax.nn.sigmoid(y)).astype(BF16)


def _conv_module(proj, w_dw, b_dw, ln_g, ln_b, S):
    T = proj.shape[0]
    tm = 512
    per_batch = S // tm
    C = CONV_CH
    ca, cg = COL_UB // C, COL_UB // C + 1
    halo_blocks = tm // CONV_HALO

    def halo_idx(col):
        return lambda i: (jnp.maximum(i * halo_blocks - 1, 0), col)

    return pl.pallas_call(
        functools.partial(_conv_kernel, tm=tm, per_batch=per_batch, rows=128),
        grid=(T // tm,),
        in_specs=[pl.BlockSpec((tm, C), lambda i: (i, ca)),
                  pl.BlockSpec((tm, C), lambda i: (i, cg)),
                  pl.BlockSpec((CONV_HALO, C), halo_idx(ca)),
                  pl.BlockSpec((CONV_HALO, C), halo_idx(cg)),
                  pl.BlockSpec((CONV_WIDTH, C), lambda i: (0, 0)),
                  pl.BlockSpec((1, C), lambda i: (0, 0)),
                  pl.BlockSpec((1, C), lambda i: (0, 0)),
                  pl.BlockSpec((1, C), lambda i: (0, 0))],
        out_specs=pl.BlockSpec((tm, C), lambda i: (i, 0)),
        out_shape=jax.ShapeDtypeStruct((T, C), BF16),
        scratch_shapes=[pltpu.VMEM((tm + CONV_HALO, C), F32)],
        compiler_params=_cparams(("arbitrary",)),
        name="conv_module",
    )(proj, proj, proj, proj, w_dw, b_dw, ln_g, ln_b)


def _merge_kernel(oa_ref, cb_ref, oc_ref, g0_ref, g1_ref, g2_ref, x_ref, gm_ref,
                  wa_ref, wb_ref, bb_ref, wc_ref, wo_ref, o_ref):
    y_a = _dot(oa_ref[...], wa_ref[...])
    y_b = _dot(cb_ref[...], wb_ref[...]) + bb_ref[...]
    y_c = _dot(oc_ref[...], wc_ref[...])
    merged = (jax.nn.sigmoid(g0_ref[...].astype(F32)) * y_a
              + jax.nn.sigmoid(g1_ref[...].astype(F32)) * y_b
              + jax.nn.sigmoid(g2_ref[...].astype(F32)) * y_c)
    o_ref[...] = x_ref[...] + gm_ref[0] * _dot(merged.astype(BF16), wo_ref[...])


def _merge(o_a, cb, o_c, proj, x, g_m, wa, wb, bb, wc, wo, S):
    T, D = x.shape
    tm = 512
    per_batch = S // tm
    W = DA_WIDTH
    gcol = COL_GATES // D
    branch = pl.BlockSpec((tm, W), lambda i: (i, 0))
    wspec = pl.BlockSpec((W, D), lambda i: (0, 0))
    return pl.pallas_call(
        _merge_kernel,
        grid=(T // tm,),
        in_specs=[branch, branch, branch,
                  pl.BlockSpec((tm, D), lambda i: (i, gcol)),
                  pl.BlockSpec((tm, D), lambda i: (i, gcol + 1)),
                  pl.BlockSpec((tm, D), lambda i: (i, gcol + 2)),
                  pl.BlockSpec((tm, D), lambda i: (i, 0)),
                  pl.BlockSpec((1, 1, D), lambda i: (i // per_batch, 0, 0)),
                  wspec, wspec, pl.BlockSpec((1, D), lambda i: (0, 0)), wspec,
                  pl.BlockSpec((D, D), lambda i: (0, 0))],
        out_specs=pl.BlockSpec((tm, D), lambda i: (i, 0)),
        out_shape=jax.ShapeDtypeStruct((T, D), F32),
        compiler_params=_cparams(("arbitrary",)),
        name="merge",
    )(o_a, cb, o_c, proj, proj, proj, x, g_m, wa, wb, bb, wc, wo)


def _first_index_of_max(vals, idx, size):
    mx = jnp.max(vals, axis=0, keepdims=True)
    first = jnp.min(jnp.where(vals == mx, idx, size), axis=0, keepdims=True)
    return mx, first


def _router_kernel(x_ref, g_ref, sc_ref, sh_ref, wh_ref, wl_ref, b_ref, before_ref, lower_ref,
                   h_ref, slot_ref, gate_ref, count_ref):
    h = _ada_rms(x_ref[...], g_ref[...], sc_ref[0], sh_ref[0])
    h_ref[...] = h.astype(BF16)
    h_hi, h_lo = _split_hi_lo(h)
    logits = _dot_nt(wh_ref[...], h_hi) + _dot_nt(wh_ref[...], h_lo) + _dot_nt(wl_ref[...], h_hi)
    scores = jax.nn.sigmoid(logits)
    biased = scores + b_ref[...]
    tm = scores.shape[1]
    minus_inf = -jnp.inf

    in_group = lax.broadcasted_iota(jnp.int32, (GROUP_SIZE, tm), 0)
    group_scores = []
    for g in range(N_GROUPS):
        vals = biased[g * GROUP_SIZE:(g + 1) * GROUP_SIZE]
        m1, i1 = _first_index_of_max(vals, in_group, GROUP_SIZE)
        m2 = jnp.max(jnp.where(in_group == i1, minus_inf, vals), axis=0, keepdims=True)
        group_scores.append(m1 + m2)
    gs = jnp.concatenate(group_scores, axis=0)

    gidx = lax.broadcasted_iota(jnp.int32, (N_GROUPS, tm), 0)
    group_sel = jnp.zeros((N_GROUPS, tm), jnp.bool_)
    for _ in range(TOPK_GROUPS):
        _, first = _first_index_of_max(gs, gidx, N_GROUPS)
        pick = gidx == first
        group_sel = jnp.logical_or(group_sel, pick)
        gs = jnp.where(pick, minus_inf, gs)

    eidx = lax.broadcasted_iota(jnp.int32, (N_EXPERTS, tm), 0)
    expert_group_sel = jnp.concatenate(
        [jnp.broadcast_to(group_sel[g:g + 1], (GROUP_SIZE, tm)) for g in range(N_GROUPS)], axis=0)
    cand = jnp.where(expert_group_sel, biased, NEG_INF)
    picks = []
    for _ in range(TOP_K):
        _, first = _first_index_of_max(cand, eidx, N_EXPERTS)
        pick = eidx == first
        picks.append(pick)
        cand = jnp.where(pick, minus_inf, cand)

    chosen = jnp.zeros((N_EXPERTS, tm), F32)
    for pick in picks:
        chosen = jnp.where(pick, 1.0, chosen)
    rank = _dot(chosen.astype(BF16), before_ref[...])
    count = jnp.sum(chosen, axis=1, keepdims=True).astype(jnp.int32)
    padded = (count + (SEG_ALIGN - 1)) & (-SEG_ALIGN)
    seg_start = _dot(lower_ref[...], jnp.broadcast_to(padded, (N_EXPERTS, LANES)).astype(BF16))[:, 0:1]
    slot_of = seg_start + rank
    slots = [jnp.sum(jnp.where(pick, slot_of, 0.0), axis=0, keepdims=True) for pick in picks]
    ws = [jnp.sum(jnp.where(pick, scores, 0.0), axis=0, keepdims=True) for pick in picks]
    norm = ROUTED_SCALE / (sum(ws) + 1e-20)
    slot_ref[...] = jnp.concatenate(slots, axis=0).astype(jnp.int32)
    gate_ref[...] = jnp.concatenate(ws, axis=0) * norm
    count_ref[0] = jnp.broadcast_to(count, (N_EXPERTS, LANES))


def _router(x1, g, sc, sh, wr_hi, wr_lo, b_router, S):
    T, D = x1.shape
    tm = MOE_TILE
    per_batch = S // tm
    E = N_EXPERTS
    tok = jnp.arange(tm, dtype=jnp.int32)
    before = (tok[:, None] < tok[None, :]).astype(BF16)
    ex = jnp.arange(E, dtype=jnp.int32)
    lower = (ex[None, :] < ex[:, None]).astype(BF16)
    return pl.pallas_call(
        _router_kernel,
        grid=(T // tm,),
        in_specs=[pl.BlockSpec((tm, D), lambda i: (i, 0)),
                  pl.BlockSpec((1, D), lambda i: (0, 0)),
                  pl.BlockSpec((1, 1, D), lambda i: (i // per_batch, 0, 0)),
                  pl.BlockSpec((1, 1, D), lambda i: (i // per_batch, 0, 0)),
                  pl.BlockSpec((E, D), lambda i: (0, 0)),
                  pl.BlockSpec((E, D), lambda i: (0, 0)),
                  pl.BlockSpec((E, 1), lambda i: (0, 0)),
                  pl.BlockSpec((tm, tm), lambda i: (0, 0)),
                  pl.BlockSpec((E, E), lambda i: (0, 0))],
        out_specs=[pl.BlockSpec((tm, D), lambda i: (i, 0)),
                   pl.BlockSpec((TOP_K, tm), lambda i: (0, i)),
                   pl.BlockSpec((TOP_K, tm), lambda i: (0, i)),
                   pl.BlockSpec((1, E, LANES), lambda i: (i, 0, 0))],
        out_shape=[jax.ShapeDtypeStruct((T, D), BF16),
                   jax.ShapeDtypeStruct((TOP_K, T), jnp.int32),
                   jax.ShapeDtypeStruct((TOP_K, T), F32),
                   jax.ShapeDtypeStruct((T // tm, E, LANES), jnp.int32)],
        compiler_params=_cparams(("arbitrary",)),
        name="router",
    )(x1, g, sc, sh, wr_hi, wr_lo, b_router, before, lower)


def _segment_pieces(tile, seg_start_ref, seg_dst_ref, seg_rows_ref, fn):
    def body(e, carry):
        idx = tile * N_EXPERTS + e
        local, dst, rows = seg_start_ref[idx], seg_dst_ref[idx], seg_rows_ref[idx]

        def pieces(sizes):
            for size in sizes:
                @pl.when(rows & size != 0)
                def _():
                    done = rows & (-2 * size)
                    fn(pl.multiple_of(local + done, SEG_ALIGN), pl.multiple_of(dst + done, SEG_ALIGN), size)

        pieces([s for s in SEG_PIECES if s < SEG_RARE])
        pl.when(rows >= SEG_RARE)(lambda: pieces([s for s in SEG_PIECES if s >= SEG_RARE]))
        return carry

    lax.fori_loop(0, N_EXPERTS, body, 0)


def _wait_pieces(tile, piece_count_ref, wait_one):
    for j, size in enumerate(SEG_PIECES):
        def body(_, carry, size=size):
            wait_one(size)
            return carry

        lax.fori_loop(0, piece_count_ref[tile * len(SEG_PIECES) + j], body, 0)


def _dispatch_kernel(seg_start_ref, seg_dst_ref, seg_rows_ref, piece_count_ref, h_ref, slot_ref, xs_in_ref, xs_ref,
                     xloc, sem, *, chunk):
    del xs_in_ref
    tile = pl.program_id(0)
    buf = tile % 2
    slots = slot_ref[...]
    h = h_ref[...]
    tm = h.shape[0]
    for r0 in range(0, xloc.shape[1], chunk):
        s = lax.broadcasted_iota(jnp.int32, (chunk, tm), 0) + r0
        onehot = jnp.zeros((chunk, tm), F32)
        for k in range(TOP_K):
            onehot = jnp.where(s == slots[k:k + 1, :], 1.0, onehot)
        xloc[buf, r0:r0 + chunk, :] = _dot(onehot.astype(BF16), h).astype(BF16)

    def copy(b, local, dst, rows):
        return pltpu.make_async_copy(xloc.at[b, pl.ds(local, rows)], xs_ref.at[pl.ds(dst, rows)], sem.at[b])

    _segment_pieces(tile, seg_start_ref, seg_dst_ref, seg_rows_ref, lambda *a: copy(buf, *a).start())

    @pl.when(tile > 0)
    def _():
        _wait_pieces(tile - 1, piece_count_ref, lambda rows: copy(1 - buf, 0, 0, rows).wait())

    @pl.when(tile == pl.num_programs(0) - 1)
    def _():
        _wait_pieces(tile, piece_count_ref, lambda rows: copy(buf, 0, 0, rows).wait())


def _dispatch(seg_start, seg_dst, seg_rows, piece_count, h2, slots, xs_buffer):
    T, D = h2.shape
    tm = MOE_TILE
    return pl.pallas_call(
        functools.partial(_dispatch_kernel, chunk=512),
        grid_spec=pltpu.PrefetchScalarGridSpec(
            num_scalar_prefetch=4,
            grid=(T // tm,),
            in_specs=[pl.BlockSpec((tm, D), lambda i, *_: (i, 0)),
                      pl.BlockSpec((TOP_K, tm), lambda i, *_: (0, i)),
                      pl.BlockSpec(memory_space=pl.ANY)],
            out_specs=pl.BlockSpec(memory_space=pl.ANY),
            scratch_shapes=[pltpu.VMEM((2, LOCAL_SLOTS, D), BF16), pltpu.SemaphoreType.DMA((2,))]),
        out_shape=jax.ShapeDtypeStruct(xs_buffer.shape, BF16),
        input_output_aliases={6: 0},
        compiler_params=_cparams(("arbitrary",)),
        name="moe_dispatch",
    )(seg_start, seg_dst, seg_rows, piece_count, h2, slots, xs_buffer)


def _expert_kernel(blk_expert_ref, blk_rows_ref, x_ref, w1_ref, w3_ref, w2_ref, y_ref):
    del blk_expert_ref
    rows = blk_rows_ref[pl.program_id(0)]

    @pl.when(rows > 0)
    def _():
        x = x_ref[...]
        a = _dot(x, w1_ref[0].astype(BF16))
        hid = a * jax.nn.sigmoid(a) * _dot(x, w3_ref[0].astype(BF16))
        y = _dot(hid.astype(BF16), w2_ref[0].astype(BF16))
        r = lax.broadcasted_iota(jnp.int32, y.shape, 0)
        y_ref[...] = jnp.where(r < rows, y, 0.0).astype(BF16)

    @pl.when(rows == 0)
    def _():
        y_ref[...] = jnp.zeros_like(y_ref)


def _experts(blk_expert, blk_rows, xs, w1, w3, w2, layer):
    n_rows, D = xs.shape
    F = EXPERT_FF
    R = EXPERT_BLOCK_ROWS
    return pl.pallas_call(
        _expert_kernel,
        grid_spec=pltpu.PrefetchScalarGridSpec(
            num_scalar_prefetch=2,
            grid=(n_rows // R,),
            in_specs=[pl.BlockSpec((R, D), lambda b, be, br: (b, 0)),
                      pl.BlockSpec((None, 1, D, F), lambda b, be, br: (layer, be[b], 0, 0)),
                      pl.BlockSpec((None, 1, D, F), lambda b, be, br: (layer, be[b], 0, 0)),
                      pl.BlockSpec((None, 1, F, D), lambda b, be, br: (layer, be[b], 0, 0))],
            out_specs=pl.BlockSpec((R, D), lambda b, be, br: (b, 0))),
        out_shape=jax.ShapeDtypeStruct((n_rows, D), BF16),
        compiler_params=_cparams(("arbitrary",)),
        name="moe_experts",
    )(blk_expert, blk_rows, xs, w1, w3, w2)


def _combine_kernel(seg_start_ref, seg_dst_ref, seg_rows_ref, piece_count_ref, ys_ref, slot_ref, gate_ref, h_ref,
                    x_ref, gf_ref, ws1_ref, ws3_ref, ws2_ref, o_ref, yloc, sem, *, chunk):
    tile = pl.program_id(0)
    buf = tile % 2

    def copy(b, local, dst, rows):
        return pltpu.make_async_copy(ys_ref.at[pl.ds(dst, rows)], yloc.at[b, pl.ds(local, rows)], sem.at[b])

    def fetch(t, b):
        yloc[b] = jnp.zeros(yloc.shape[1:], BF16)
        _segment_pieces(t, seg_start_ref, seg_dst_ref, seg_rows_ref, lambda *a: copy(b, *a).start())

    pl.when(tile == 0)(lambda: fetch(tile, buf))
    pl.when(tile + 1 < pl.num_programs(0))(lambda: fetch(tile + 1, 1 - buf))
    h = h_ref[...]
    a = _dot(h, ws1_ref[...])
    total = _dot((a * jax.nn.sigmoid(a) * _dot(h, ws3_ref[...])).astype(BF16), ws2_ref[...])
    _wait_pieces(tile, piece_count_ref, lambda rows: copy(buf, 0, 0, rows).wait())

    slots = slot_ref[...]
    gates = gate_ref[...]
    tm = h.shape[0]
    for c0 in range(0, yloc.shape[1], chunk):
        s = lax.broadcasted_iota(jnp.int32, (tm, chunk), 1) + c0
        weights = jnp.zeros((tm, chunk), F32)
        for k in range(TOP_K):
            weights = jnp.where(s == slots[:, k:k + 1], gates[:, k:k + 1], weights)
        total = total + _dot(weights.astype(BF16), yloc[buf, c0:c0 + chunk, :])
    o_ref[...] = x_ref[...] + gf_ref[0] * total


def _combine(seg_start, seg_dst, seg_rows, piece_count, ys, slots_tok, gates_tok, h2, x1, g_f, ws1, ws3, ws2, S):
    T, D = x1.shape
    tm = MOE_TILE
    per_batch = S // tm
    F = ws1.shape[1]
    return pl.pallas_call(
        functools.partial(_combine_kernel, chunk=512),
        grid_spec=pltpu.PrefetchScalarGridSpec(
            num_scalar_prefetch=4,
            grid=(T // tm,),
            in_specs=[pl.BlockSpec(memory_space=pl.ANY),
                      pl.BlockSpec((tm, TOP_K), lambda i, *_: (i, 0)),
                      pl.BlockSpec((tm, TOP_K), lambda i, *_: (i, 0)),
                      pl.BlockSpec((tm, D), lambda i, *_: (i, 0)),
                      pl.BlockSpec((tm, D), lambda i, *_: (i, 0)),
                      pl.BlockSpec((1, 1, D), lambda i, *_: (i // per_batch, 0, 0)),
                      pl.BlockSpec((D, F), lambda i, *_: (0, 0)),
                      pl.BlockSpec((D, F), lambda i, *_: (0, 0)),
                      pl.BlockSpec((F, D), lambda i, *_: (0, 0))],
            out_specs=pl.BlockSpec((tm, D), lambda i, *_: (i, 0)),
            scratch_shapes=[pltpu.VMEM((2, LOCAL_SLOTS, D), BF16), pltpu.SemaphoreType.DMA((2,))]),
        out_shape=jax.ShapeDtypeStruct((T, D), F32),
        compiler_params=_cparams(("arbitrary",)),
        name="moe_combine",
    )(seg_start, seg_dst, seg_rows, piece_count, ys, slots_tok, gates_tok, h2, x1, g_f, ws1, ws3, ws2)


def _moe_buffer_rows(T):
    R = EXPERT_BLOCK_ROWS
    n_segments = (T // MOE_TILE) * N_EXPERTS
    return -(-(T * TOP_K + n_segments * (SEG_ALIGN - 1) + N_EXPERTS * (R - SEG_ALIGN)) // R) * R


def _moe(x1, norm_g, sc_f, sh_f, g_f, w_router, b_router, w1, w3, w2, layer, ws1, ws3, ws2, S, xs_buffer):
    T, D = x1.shape
    E = N_EXPERTS
    R = EXPERT_BLOCK_ROWS
    wr_t = w_router.T
    wr_hi = wr_t.astype(BF16)
    wr_lo = (wr_t - wr_hi.astype(F32)).astype(BF16)
    h2, slots, gates, counts = _router(x1, norm_g.reshape(1, D), sc_f, sh_f, wr_hi, wr_lo, b_router.reshape(E, 1), S)

    seg_rows = (counts[:, :, 0] + (SEG_ALIGN - 1)) // SEG_ALIGN * SEG_ALIGN
    seg_start = jnp.cumsum(seg_rows, axis=1) - seg_rows
    expert_rows = jnp.sum(seg_rows, axis=0)
    region = (expert_rows + (R - 1)) // R * R
    region_end = jnp.cumsum(region)
    region_start = region_end - region
    seg_dst = region_start[None, :] + jnp.cumsum(seg_rows, axis=0) - seg_rows
    blk_first = jnp.arange(xs_buffer.shape[0] // R, dtype=jnp.int32) * R
    blk_expert = jnp.minimum(jnp.sum(blk_first[:, None] >= region_end[None, :], axis=1), E - 1).astype(jnp.int32)
    blk_rows = jnp.clip(expert_rows[blk_expert] - (blk_first - region_start[blk_expert]), 0, R).astype(jnp.int32)

    piece_count = jnp.stack([jnp.sum((seg_rows & size) != 0, axis=1) for size in SEG_PIECES], axis=1)

    flat = lambda a: a.reshape(-1).astype(jnp.int32)
    tables = (flat(seg_start), flat(seg_dst), flat(seg_rows), flat(piece_count))
    xs = _dispatch(*tables, h2, slots, xs_buffer)
    ys = _experts(blk_expert, blk_rows, xs, w1, w3, w2, layer)
    out = _combine(*tables, ys, slots.T, gates.T, h2, x1, g_f,
                   ws1.astype(BF16), ws3.astype(BF16), ws2.astype(BF16), S)
    return out, xs


def kernel(x, c, positions, w_mod, b_mod, norm_mix_g, norm_ffn_g, w_in, qn_g, kn_g, lam_q1, lam_k1, lam_q2, lam_k2, subln_g, w_proj_a, w_dw, b_dw, conv_ln_g, conv_ln_b, w_proj_b, b_proj_b, w_proj_c, w_out, w_router, b_router, w1, w3, w2, ws1, ws3, ws2):
    B, S, D = x.shape
    T = B * S
    depth = w_mod.shape[0]
    xt = x.reshape(T, D)

    inv = ROPE_THETA ** (-jnp.arange(0, DA_HEAD_DIM, 2, dtype=F32) / DA_HEAD_DIM)
    inv_lanes = jnp.tile(inv, LANES // (DA_HEAD_DIM // 2)).reshape(1, LANES)
    pos_lanes = jnp.broadcast_to(positions.astype(F32).reshape(T, 1), (T, LANES))
    cos, sin = _rope_tables(pos_lanes, inv_lanes)

    mod = _modulation(c, w_mod, b_mod)

    seg = jnp.arange(DA_WIDTH, dtype=jnp.int32) // DA_HEAD_DIM
    bd = jnp.where(seg[:, None] == seg[None, :], 1.0 / DA_HEAD_DIM, 0.0).astype(BF16)
    kk = jnp.arange(SB_BLOCK, dtype=jnp.int32)
    u = jnp.where(kk[:, None] >= kk[None, :], -1.0, 0.0).astype(BF16)
    u = jnp.concatenate([u, u], axis=0)
    n_seg = DA_WIDTH // DA_HEAD_DIM

    xs_buffer = jnp.zeros((_moe_buffer_rows(T), D), BF16)
    for l in range(depth):
        lambda_init = 0.8 - 0.6 * math.exp(-0.3 * l)
        sh_m, sc_m, g_m, sh_f, sc_f, g_f = [m.reshape(B, 1, D) for m in jnp.split(mod[l], 6, axis=-1)]

        proj = _norm_proj(xt, norm_mix_g[l].reshape(1, D), sc_m, sh_m, w_in[l].astype(BF16), S)

        q_r, k_r = _qk_prep(proj, cos, sin, bd,
                            jnp.tile(qn_g[l], n_seg).reshape(1, DA_WIDTH),
                            jnp.tile(kn_g[l], n_seg).reshape(1, DA_WIDTH))
        lam = (jnp.exp(jnp.sum(lam_q1[l] * lam_k1[l])) - jnp.exp(jnp.sum(lam_q2[l] * lam_k2[l])) + lambda_init)
        score_bound = (DA_HEAD_DIM ** 0.5) * jnp.max(jnp.abs(qn_g[l])) * jnp.max(jnp.abs(kn_g[l]))
        attn = functools.partial(_diff_attn, B=B, S=S, out_scale=1.0 - lambda_init)
        o_a = lax.cond(score_bound <= MAX_FIXED_SHIFT,
                       functools.partial(attn, online=False), functools.partial(attn, online=True),
                       score_bound.reshape(1, 1).astype(F32), jnp.full((1, LANES), lam, F32), q_r, k_r, proj,
                       subln_g[l].reshape(1, LANES))
        cb = _conv_module(proj, w_dw[l], b_dw[l].reshape(1, CONV_CH), conv_ln_g[l].reshape(1, CONV_CH),
                          conv_ln_b[l].reshape(1, CONV_CH), S)
        o_c = _sb_attn(proj, u, B, S)
        x1 = _merge(o_a, cb, o_c, proj, xt, g_m, w_proj_a[l].astype(BF16), w_proj_b[l].astype(BF16),
                    b_proj_b[l].reshape(1, D), w_proj_c[l].astype(BF16), w_out[l].astype(BF16), S)

        xt, xs_buffer = _moe(x1, norm_ffn_g[l], sc_f, sh_f, g_f, w_router[l], b_router[l],
                             w1, w3, w2, l, ws1[l], ws3[l], ws2[l], S, xs_buffer)

    return xt.reshape(B, S, D)
```

```python
import functools
import math

import jax
import jax.numpy as jnp
from jax import lax
from jax.experimental import pallas as pl
from jax.experimental.pallas import tpu as pltpu

F32 = jnp.float32
BF16 = jnp.bfloat16

D_MODEL = 1024
DA_HEADS = 4
DA_HEAD_DIM = 64
DA_WIDTH = 512
CONV_CH = 512
CONV_WIDTH = 31
SB_WIDTH = 512
SB_HEAD_DIM = 64
IN_COLS = 7168
ROPE_THETA = 10000.0
EPS = 1e-6
NEG_INF = -1e30
LOG2E = math.log2(math.e)
MAX_FIXED_SHIFT = 40.0
SB_BLOCK = 256
SB_BLOCKS_PER_TRIP = 4
N_EXPERTS = 64
TOP_K = 8
N_GROUPS = 8
TOPK_GROUPS = 4
GROUP_SIZE = N_EXPERTS // N_GROUPS
EXPERT_FF = 256
ROUTED_SCALE = 2.5

MOE_TILE = 512
SEG_ALIGN = 16
SEG_PIECES = tuple(SEG_ALIGN << j for j in reversed(range((MOE_TILE // SEG_ALIGN).bit_length())))
LOCAL_SLOTS = -(-(MOE_TILE * TOP_K + N_EXPERTS * (SEG_ALIGN - 1)) // 512) * 512
SEG_RARE = 128
EXPERT_BLOCK_ROWS = 1024
LANES = 128
SUBLANES = 8
CONV_HALO = 32

COL_QA, COL_KA, COL_VA, COL_UB, COL_QC, COL_KC, COL_VC, COL_GATES = 0, 512, 1024, 1536, 2560, 3072, 3584, 4096

VMEM_LIMIT = 48 * 1024 * 1024


def _cparams(sem):
    return pltpu.CompilerParams(dimension_semantics=sem, vmem_limit_bytes=VMEM_LIMIT)


def _dot(a, b):
    return jnp.dot(a, b, preferred_element_type=F32)


def _dot_nt(a, b):
    return lax.dot_general(a, b, (((1,), (1,)), ((), ())), preferred_element_type=F32)


def _split_hi_lo(x):
    hi = x.astype(BF16)
    lo = (x - hi.astype(F32)).astype(BF16)
    return hi, lo


def _mod_kernel(c_ref, w_ref, b_ref, o_ref):
    c = c_ref[...]
    c_act = c * jax.nn.sigmoid(c)
    o_ref[0] = jnp.dot(c_act, w_ref[0], precision=lax.Precision.HIGHEST,
                       preferred_element_type=F32) + b_ref[0]


def _modulation(c, w_mod, b_mod):
    L, D, N = w_mod.shape
    B = c.shape[0]
    tn = 1536
    return pl.pallas_call(
        _mod_kernel,
        grid=(L, N // tn),
        in_specs=[pl.BlockSpec((B, D), lambda l, j: (0, 0)),
                  pl.BlockSpec((1, D, tn), lambda l, j: (l, 0, j)),
                  pl.BlockSpec((1, 1, tn), lambda l, j: (l, 0, j))],
        out_specs=pl.BlockSpec((1, B, tn), lambda l, j: (l, 0, j)),
        out_shape=jax.ShapeDtypeStruct((L, B, N), F32),
        compiler_params=_cparams(("arbitrary", "arbitrary")),
        name="modulation",
    )(c, w_mod, b_mod.reshape(L, 1, N))


def _rope_kernel(pos_ref, inv_ref, cos_ref, sin_ref):
    ang = pos_ref[...] * inv_ref[...]
    lane = lax.broadcasted_iota(jnp.int32, ang.shape, 1)
    first_half = (lane % DA_HEAD_DIM) < (DA_HEAD_DIM // 2)
    cos_ref[...] = jnp.cos(ang)
    s = jnp.sin(ang)
    sin_ref[...] = jnp.where(first_half, -s, s)


def _rope_tables(pos_lanes, inv_lanes):
    T = pos_lanes.shape[0]
    tm = 2048
    return pl.pallas_call(
        _rope_kernel,
        grid=(T // tm,),
        in_specs=[pl.BlockSpec((tm, LANES), lambda i: (i, 0)),
                  pl.BlockSpec((1, LANES), lambda i: (0, 0))],
        out_specs=[pl.BlockSpec((tm, LANES), lambda i: (i, 0))] * 2,
        out_shape=[jax.ShapeDtypeStruct((T, LANES), F32)] * 2,
        compiler_params=_cparams(("arbitrary",)),
        name="rope_tables",
    )(pos_lanes, inv_lanes)


def _ada_rms(x, g, sc, sh):
    ms = jnp.mean(x * x, axis=-1, keepdims=True)
    return x * lax.rsqrt(ms + EPS) * g * (1.0 + sc) + sh


def _norm_proj_kernel(x_ref, g_ref, sc_ref, sh_ref, w_ref, o_ref, h_scr):
    @pl.when(pl.program_id(1) == 0)
    def _():
        h_scr[...] = _ada_rms(x_ref[...], g_ref[...], sc_ref[0], sh_ref[0]).astype(BF16)

    o_ref[...] = _dot(h_scr[...], w_ref[...]).astype(BF16)


def _norm_proj(x, g, sc, sh, w_bf16, S):
    T, D = x.shape
    N = w_bf16.shape[1]
    tm, tn = 2048, 1024
    per_batch = S // tm
    return pl.pallas_call(
        _norm_proj_kernel,
        grid=(T // tm, N // tn),
        in_specs=[pl.BlockSpec((tm, D), lambda i, j: (i, 0)),
                  pl.BlockSpec((1, D), lambda i, j: (0, 0)),
                  pl.BlockSpec((1, 1, D), lambda i, j: (i // per_batch, 0, 0)),
                  pl.BlockSpec((1, 1, D), lambda i, j: (i // per_batch, 0, 0)),
                  pl.BlockSpec((D, tn), lambda i, j: (0, j))],
        out_specs=pl.BlockSpec((tm, tn), lambda i, j: (i, j)),
        out_shape=jax.ShapeDtypeStruct((T, N), BF16),
        scratch_shapes=[pltpu.VMEM((tm, D), BF16)],
        compiler_params=_cparams(("arbitrary", "arbitrary")),
        name="norm_proj",
    )(x, g, sc, sh, w_bf16)


def _qk_prep_kernel(q_ref, k_ref, cos_ref, sin_ref, bd_ref, qg_ref, kg_ref, qo_ref, ko_ref):
    cos = cos_ref[...]
    sin = sin_ref[...]
    bd = bd_ref[...]
    lane = lax.broadcasted_iota(jnp.int32, cos.shape, 1)
    first_half = (lane % DA_HEAD_DIM) < (DA_HEAD_DIM // 2)
    half = DA_HEAD_DIM // 2

    def prep(x_ref, g_ref, o_ref, scale):
        x = x_ref[...].astype(F32)
        hi, lo = _split_hi_lo(x * x)
        ms = _dot(hi, bd) + _dot(lo, bd)
        y = x * lax.rsqrt(ms + EPS) * g_ref[...]
        for c in range(DA_WIDTH // LANES):
            yc = y[:, c * LANES:(c + 1) * LANES]
            swapped = jnp.where(first_half, pltpu.roll(yc, LANES - half, 1), pltpu.roll(yc, half, 1))
            o_ref[:, c * LANES:(c + 1) * LANES] = ((yc * cos + swapped * sin) * scale).astype(BF16)

    prep(q_ref, qg_ref, qo_ref, DA_HEAD_DIM ** -0.5)
    prep(k_ref, kg_ref, ko_ref, 1.0)


def _qk_prep(proj, cos, sin, bd, qg, kg):
    T = proj.shape[0]
    tm = 1024
    W = DA_WIDTH
    return pl.pallas_call(
        _qk_prep_kernel,
        grid=(T // tm,),
        in_specs=[pl.BlockSpec((tm, W), lambda i: (i, COL_QA // W)),
                  pl.BlockSpec((tm, W), lambda i: (i, COL_KA // W)),
                  pl.BlockSpec((tm, LANES), lambda i: (i, 0)),
                  pl.BlockSpec((tm, LANES), lambda i: (i, 0)),
                  pl.BlockSpec((W, W), lambda i: (0, 0)),
                  pl.BlockSpec((1, W), lambda i: (0, 0)),
                  pl.BlockSpec((1, W), lambda i: (0, 0))],
        out_specs=[pl.BlockSpec((tm, W), lambda i: (i, 0))] * 2,
        out_shape=[jax.ShapeDtypeStruct((T, W), BF16)] * 2,
        compiler_params=_cparams(("arbitrary",)),
        name="qk_prep",
    )(proj, proj, cos, sin, bd, qg, kg)


def _diff_attn_kernel(shift_ref, lam_ref, q_ref, k_ref, v_ref, g_ref, o_ref, acc1, acc2, *, tq, out_scale, online):
    qi = pl.program_id(2)
    qf = q_ref[...].astype(F32)
    lane = lax.broadcasted_iota(jnp.int32, qf.shape, 1)
    qs = (jnp.where(lane < DA_HEAD_DIM, qf, 0.0).astype(BF16),
          jnp.where(lane >= DA_HEAD_DIM, qf, 0.0).astype(BF16))
    accs = (acc1, acc2)
    acc1[...] = jnp.zeros_like(acc1)
    acc2[...] = jnp.zeros_like(acc2)
    row = lax.broadcasted_iota(jnp.int32, (tq, tq), 0)
    col = lax.broadcasted_iota(jnp.int32, (tq, tq), 1)
    causal = col <= row
    shift = shift_ref[...]

    def step(kj, carry, masked):
        start = pl.multiple_of(kj * tq, tq)
        k = k_ref[pl.ds(start, tq), :]
        v = v_ref[pl.ds(start, tq), :]
        ss = [_dot_nt(q, k) for q in qs]
        if masked:
            ss = [jnp.where(causal, s, NEG_INF) for s in ss]
        if online:
            ms, ls = carry
            m_new = [jnp.maximum(m, jnp.max(s, axis=1, keepdims=True)) for m, s in zip(ms, ss)]
            ps = [jnp.exp(s - m) for s, m in zip(ss, m_new)]
            alphas = [jnp.exp(m - mn) for m, mn in zip(ms, m_new)]
            for acc, alpha, p in zip(accs, alphas, ps):
                acc[...] = alpha * acc[...] + _dot(p.astype(BF16), v)
            ls = [alpha * l + jnp.sum(p, axis=1, keepdims=True) for alpha, l, p in zip(alphas, ls, ps)]
            return tuple(m_new), tuple(ls)
        ps = [jnp.exp(s - shift) for s in ss]
        for acc, p in zip(accs, ps):
            acc[...] += _dot(p.astype(BF16), v)
        return tuple(l + jnp.sum(p, axis=1, keepdims=True) for l, p in zip(carry, ps))

    zero = jnp.zeros((tq, 1), F32)
    if online:
        neg = jnp.full((tq, 1), NEG_INF, F32)
        init = ((neg, neg), (zero, zero))
    else:
        init = (zero, zero)
    if online:
        carry = lax.fori_loop(0, qi, lambda kj, c: step(kj, c, False), init)
    else:
        def pair_step(t, ls):
            starts = [pl.multiple_of((2 * t + d) * tq, tq) for d in range(2)]
            ks = [k_ref[pl.ds(st, tq), :] for st in starts]
            vs = [v_ref[pl.ds(st, tq), :] for st in starts]
            ps = [[jnp.exp(_dot_nt(q, k) - shift) for q in qs] for k in ks]
            for blk, v in zip(ps, vs):
                for acc, p in zip(accs, blk):
                    acc[...] += _dot(p.astype(BF16), v)
            return tuple(l + jnp.sum(ps[0][m], axis=1, keepdims=True) + jnp.sum(ps[1][m], axis=1, keepdims=True)
                         for m, l in enumerate(ls))

        carry = lax.fori_loop(0, qi // 2, pair_step, init)
        carry = lax.cond(qi % 2 == 1, lambda c: step(qi - 1, c, False), lambda c: c, carry)
    carry = step(qi, carry, True)
    l1, l2 = carry[1] if online else carry
    o = acc1[...] / l1 - lam_ref[...] * (acc2[...] / l2)
    ms = jnp.mean(o * o, axis=-1, keepdims=True)
    o_ref[...] = (o * lax.rsqrt(ms + EPS) * g_ref[...] * out_scale).astype(BF16)


def _diff_attn(shift, lam_lanes, q_r, k_r, proj, subln_g, B, S, out_scale, online):
    T = q_r.shape[0]
    tq = 512
    nq = S // tq
    return pl.pallas_call(
        functools.partial(_diff_attn_kernel, tq=tq, out_scale=out_scale, online=online),
        grid=(B, DA_HEADS, nq),
        in_specs=[pl.BlockSpec((1, 1), lambda b, h, i: (0, 0)),
                  pl.BlockSpec((1, LANES), lambda b, h, i: (0, 0)),
                  pl.BlockSpec((tq, LANES), lambda b, h, i: (b * nq + i, h)),
                  pl.BlockSpec((S, LANES), lambda b, h, i: (b, h)),
                  pl.BlockSpec((S, LANES), lambda b, h, i: (b, COL_VA // LANES + h)),
                  pl.BlockSpec((1, LANES), lambda b, h, i: (0, 0))],
        out_specs=pl.BlockSpec((tq, LANES), lambda b, h, i: (b * nq + i, h)),
        out_shape=jax.ShapeDtypeStruct((T, DA_WIDTH), BF16),
        scratch_shapes=[pltpu.VMEM((tq, LANES), F32), pltpu.VMEM((tq, LANES), F32)],
        compiler_params=_cparams(("arbitrary", "arbitrary", "arbitrary")),
        name="diff_attn_online" if online else "diff_attn",
    )(shift, lam_lanes, q_r, k_r, proj, subln_g)


def _sb_attn_kernel(q_ref, k_ref, v_ref, u_ref, o_ref, acc, *, tq):
    qi = pl.program_id(2)
    qf = q_ref[...].astype(F32) * (SB_HEAD_DIM ** -0.5 * LOG2E)
    lane = lax.broadcasted_iota(jnp.int32, qf.shape, 1)
    qs = (jnp.where(lane < SB_HEAD_DIM, qf, 0.0).astype(BF16),
          jnp.where(lane >= SB_HEAD_DIM, qf, 0.0).astype(BF16))
    uu = u_ref[...]
    acc[...] = jnp.zeros_like(acc)
    row = lax.broadcasted_iota(jnp.int32, (tq, tq), 0)
    col = lax.broadcasted_iota(jnp.int32, (tq, tq), 1)
    before = col < row

    def step(kjs, rs, masks):
        ks = [k_ref[pl.ds(pl.multiple_of(kj * tq, tq), tq), :] for kj in kjs]
        vs = [v_ref[pl.ds(pl.multiple_of(kj * tq, tq), tq), :] for kj in kjs]
        chains = [(h, b) for b in range(len(kjs)) for h in range(2)]
        zs = [_dot_nt(qs[h], ks[b]) for h, b in chains]
        ps, hls = [], []
        for z, (_, b) in zip(zs, chains):
            neg_abs = pltpu.bitcast(pltpu.bitcast(z, jnp.uint32) | jnp.uint32(0x80000000), F32)
            p = jnp.maximum(z, 0.0) + jnp.log2(1.0 + jnp.exp2(neg_abs))
            if masks[b]:
                p = jnp.where(before, p, 0.0)
            hi, lo = _split_hi_lo(p)
            ps.append(jnp.sum(p, axis=1, keepdims=True))
            hls.append(jnp.concatenate([hi, lo], axis=1))
        mms = [_dot(hl, uu) for hl in hls]
        rs = list(rs)
        for c, (h, b) in enumerate(chains):
            a = jnp.exp2(zs[c] + mms[c] + rs[h])
            if masks[b]:
                a = jnp.where(before, a, 0.0)
            acc[h * tq:(h + 1) * tq, :] += _dot(a.astype(BF16), vs[b])
            rs[h] = rs[h] - ps[c]
        return tuple(rs)

    zero = jnp.zeros((tq, 1), F32)
    rs = lax.cond(qi % 2 == 1,
                  lambda c: step([qi, qi - 1], c, (True, False)),
                  lambda c: step([qi], c, (True,)), (zero, zero))
    full = qi - qi % 2

    def run(first, count, c):
        return step([first - d for d in range(count)], c, (False,) * count)

    nb = SB_BLOCKS_PER_TRIP
    rs = lax.fori_loop(0, full // nb, lambda t, c: run(full - 1 - nb * t, nb, c), rs)
    left = full % nb
    part = nb // 2
    while part >= 2:
        rs = lax.cond(left & part != 0, functools.partial(run, (left & (2 * part - 1)) - 1, part),
                      lambda c: c, rs)
        part //= 2

    o_ref[...] = jnp.where(lane < SB_HEAD_DIM, acc[0:tq, :], acc[tq:2 * tq, :]).astype(BF16)


def _sb_attn(proj, u, B, S):
    T = proj.shape[0]
    tq = SB_BLOCK
    nq = S // tq
    return pl.pallas_call(
        functools.partial(_sb_attn_kernel, tq=tq),
        grid=(B, SB_WIDTH // LANES, nq),
        in_specs=[pl.BlockSpec((tq, LANES), lambda b, p, i: (b * nq + i, COL_QC // LANES + p)),
                  pl.BlockSpec((S, LANES), lambda b, p, i: (b, COL_KC // LANES + p)),
                  pl.BlockSpec((S, LANES), lambda b, p, i: (b, COL_VC // LANES + p)),
                  pl.BlockSpec((2 * tq, tq), lambda b, p, i: (0, 0))],
        out_specs=pl.BlockSpec((tq, LANES), lambda b, p, i: (b * nq + i, p)),
        out_shape=jax.ShapeDtypeStruct((T, SB_WIDTH), BF16),
        scratch_shapes=[pltpu.VMEM((2 * tq, LANES), F32)],
        compiler_params=_cparams(("arbitrary", "arbitrary", "arbitrary")),
        name="sb_attn",
    )(proj, proj, proj, u)


def _conv_kernel(a_ref, g_ref, ah_ref, gh_ref, w_ref, b_ref, lg_ref, lb_ref, o_ref, hbuf, *, tm, per_batch, rows):
    i = pl.program_id(0)
    halo = ah_ref[...].astype(F32) * jax.nn.sigmoid(gh_ref[...].astype(F32))
    hbuf[0:CONV_HALO, :] = jnp.where(i % per_batch == 0, 0.0, halo)
    hbuf[CONV_HALO:, :] = a_ref[...].astype(F32) * jax.nn.sigmoid(g_ref[...].astype(F32))
    w = w_ref[...]
    for r0 in range(0, tm, rows):
        acc = jnp.broadcast_to(b_ref[...], (rows, CONV_CH))
        for b in range(SUBLANES):
            part = None
            for a in range((CONV_WIDTH - 1 - b) // SUBLANES + 1):
                lo = r0 + CONV_HALO - SUBLANES - SUBLANES * a
                tap = CONV_WIDTH - 1 - SUBLANES * a - b
                term = hbuf[lo:lo + rows + SUBLANES, :] * w[tap:tap + 1, :]
                part = term if part is None else part + term
            acc = acc + part[SUBLANES - b:SUBLANES - b + rows, :]
        mu = jnp.mean(acc, axis=-1, keepdims=True)
        d = acc - mu
        var = jnp.mean(d * d, axis=-1, keepdims=True)
        y = d * lax.rsqrt(var + EPS) * lg_ref[...] + lb_ref[...]
        o_ref[r0:r0 + rows, :] = (y * jax.nn.sigmoid(y)).astype(BF16)


def _conv_module(proj, w_dw, b_dw, ln_g, ln_b, S):
    T = proj.shape[0]
    tm = 512
    per_batch = S // tm
    C = CONV_CH
    ca, cg = COL_UB // C, COL_UB // C + 1
    halo_blocks = tm // CONV_HALO

    def halo_idx(col):
        return lambda i: (jnp.maximum(i * halo_blocks - 1, 0), col)

    return pl.pallas_call(
        functools.partial(_conv_kernel, tm=tm, per_batch=per_batch, rows=128),
        grid=(T // tm,),
        in_specs=[pl.BlockSpec((tm, C), lambda i: (i, ca)),
                  pl.BlockSpec((tm, C), lambda i: (i, cg)),
                  pl.BlockSpec((CONV_HALO, C), halo_idx(ca)),
                  pl.BlockSpec((CONV_HALO, C), halo_idx(cg)),
                  pl.BlockSpec((CONV_WIDTH, C), lambda i: (0, 0)),
                  pl.BlockSpec((1, C), lambda i: (0, 0)),
                  pl.BlockSpec((1, C), lambda i: (0, 0)),
                  pl.BlockSpec((1, C), lambda i: (0, 0))],
        out_specs=pl.BlockSpec((tm, C), lambda i: (i, 0)),
        out_shape=jax.ShapeDtypeStruct((T, C), BF16),
        scratch_shapes=[pltpu.VMEM((tm + CONV_HALO, C), F32)],
        compiler_params=_cparams(("arbitrary",)),
        name="conv_module",
    )(proj, proj, proj, proj, w_dw, b_dw, ln_g, ln_b)


def _merge_kernel(oa_ref, cb_ref, oc_ref, g0_ref, g1_ref, g2_ref, x_ref, gm_ref,
                  wa_ref, wb_ref, bb_ref, wc_ref, wo_ref, o_ref):
    y_a = _dot(oa_ref[...], wa_ref[...])
    y_b = _dot(cb_ref[...], wb_ref[...]) + bb_ref[...]
    y_c = _dot(oc_ref[...], wc_ref[...])
    merged = (jax.nn.sigmoid(g0_ref[...].astype(F32)) * y_a
              + jax.nn.sigmoid(g1_ref[...].astype(F32)) * y_b
              + jax.nn.sigmoid(g2_ref[...].astype(F32)) * y_c)
    o_ref[...] = x_ref[...] + gm_ref[0] * _dot(merged.astype(BF16), wo_ref[...])


def _merge(o_a, cb, o_c, proj, x, g_m, wa, wb, bb, wc, wo, S):
    T, D = x.shape
    tm = 512
    per_batch = S // tm
    W = DA_WIDTH
    gcol = COL_GATES // D
    branch = pl.BlockSpec((tm, W), lambda i: (i, 0))
    wspec = pl.BlockSpec((W, D), lambda i: (0, 0))
    return pl.pallas_call(
        _merge_kernel,
        grid=(T // tm,),
        in_specs=[branch, branch, branch,
                  pl.BlockSpec((tm, D), lambda i: (i, gcol)),
                  pl.BlockSpec((tm, D), lambda i: (i, gcol + 1)),
                  pl.BlockSpec((tm, D), lambda i: (i, gcol + 2)),
                  pl.BlockSpec((tm, D), lambda i: (i, 0)),
                  pl.BlockSpec((1, 1, D), lambda i: (i // per_batch, 0, 0)),
                  wspec, wspec, pl.BlockSpec((1, D), lambda i: (0, 0)), wspec,
                  pl.BlockSpec((D, D), lambda i: (0, 0))],
        out_specs=pl.BlockSpec((tm, D), lambda i: (i, 0)),
        out_shape=jax.ShapeDtypeStruct((T, D), F32),
        compiler_params=_cparams(("arbitrary",)),
        name="merge",
    )(o_a, cb, o_c, proj, proj, proj, x, g_m, wa, wb, bb, wc, wo)


def _first_index_of_max(vals, idx, size):
    mx = jnp.max(vals, axis=0, keepdims=True)
    first = jnp.min(jnp.where(vals == mx, idx, size), axis=0, keepdims=True)
    return mx, first


def _router_kernel(x_ref, g_ref, sc_ref, sh_ref, wh_ref, wl_ref, b_ref, before_ref, lower_ref,
                   h_ref, slot_ref, gate_ref, count_ref):
    h = _ada_rms(x_ref[...], g_ref[...], sc_ref[0], sh_ref[0])
    h_ref[...] = h.astype(BF16)
    h_hi, h_lo = _split_hi_lo(h)
    logits = _dot_nt(wh_ref[...], h_hi) + _dot_nt(wh_ref[...], h_lo) + _dot_nt(wl_ref[...], h_hi)
    scores = jax.nn.sigmoid(logits)
    biased = scores + b_ref[...]
    tm = scores.shape[1]
    minus_inf = -jnp.inf

    in_group = lax.broadcasted_iota(jnp.int32, (GROUP_SIZE, tm), 0)
    group_scores = []
    for g in range(N_GROUPS):
        vals = biased[g * GROUP_SIZE:(g + 1) * GROUP_SIZE]
        m1, i1 = _first_index_of_max(vals, in_group, GROUP_SIZE)
        m2 = jnp.max(jnp.where(in_group == i1, minus_inf, vals), axis=0, keepdims=True)
        group_scores.append(m1 + m2)
    gs = jnp.concatenate(group_scores, axis=0)

    gidx = lax.broadcasted_iota(jnp.int32, (N_GROUPS, tm), 0)
    group_sel = jnp.zeros((N_GROUPS, tm), jnp.bool_)
    for _ in range(TOPK_GROUPS):
        _, first = _first_index_of_max(gs, gidx, N_GROUPS)
        pick = gidx == first
        group_sel = jnp.logical_or(group_sel, pick)
        gs = jnp.where(pick, minus_inf, gs)

    eidx = lax.broadcasted_iota(jnp.int32, (N_EXPERTS, tm), 0)
    expert_group_sel = jnp.concatenate(
        [jnp.broadcast_to(group_sel[g:g + 1], (GROUP_SIZE, tm)) for g in range(N_GROUPS)], axis=0)
    cand = jnp.where(expert_group_sel, biased, NEG_INF)
    picks = []
    for _ in range(TOP_K):
        _, first = _first_index_of_max(cand, eidx, N_EXPERTS)
        pick = eidx == first
        picks.append(pick)
        cand = jnp.where(pick, minus_inf, cand)

    chosen = jnp.zeros((N_EXPERTS, tm), F32)
    for pick in picks:
        chosen = jnp.where(pick, 1.0, chosen)
    rank = _dot(chosen.astype(BF16), before_ref[...])
    count = jnp.sum(chosen, axis=1, keepdims=True).astype(jnp.int32)
    padded = (count + (SEG_ALIGN - 1)) & (-SEG_ALIGN)
    seg_start = _dot(lower_ref[...], jnp.broadcast_to(padded, (N_EXPERTS, LANES)).astype(BF16))[:, 0:1]
    slot_of = seg_start + rank
    slots = [jnp.sum(jnp.where(pick, slot_of, 0.0), axis=0, keepdims=True) for pick in picks]
    ws = [jnp.sum(jnp.where(pick, scores, 0.0), axis=0, keepdims=True) for pick in picks]
    norm = ROUTED_SCALE / (sum(ws) + 1e-20)
    slot_ref[...] = jnp.concatenate(slots, axis=0).astype(jnp.int32)
    gate_ref[...] = jnp.concatenate(ws, axis=0) * norm
    count_ref[0] = jnp.broadcast_to(count, (N_EXPERTS, LANES))


def _router(x1, g, sc, sh, wr_hi, wr_lo, b_router, S):
    T, D = x1.shape
    tm = MOE_TILE
    per_batch = S // tm
    E = N_EXPERTS
    tok = jnp.arange(tm, dtype=jnp.int32)
    before = (tok[:, None] < tok[None, :]).astype(BF16)
    ex = jnp.arange(E, dtype=jnp.int32)
    lower = (ex[None, :] < ex[:, None]).astype(BF16)
    return pl.pallas_call(
        _router_kernel,
        grid=(T // tm,),
        in_specs=[pl.BlockSpec((tm, D), lambda i: (i, 0)),
                  pl.BlockSpec((1, D), lambda i: (0, 0)),
                  pl.BlockSpec((1, 1, D), lambda i: (i // per_batch, 0, 0)),
                  pl.BlockSpec((1, 1, D), lambda i: (i // per_batch, 0, 0)),
                  pl.BlockSpec((E, D), lambda i: (0, 0)),
                  pl.BlockSpec((E, D), lambda i: (0, 0)),
                  pl.BlockSpec((E, 1), lambda i: (0, 0)),
                  pl.BlockSpec((tm, tm), lambda i: (0, 0)),
                  pl.BlockSpec((E, E), lambda i: (0, 0))],
        out_specs=[pl.BlockSpec((tm, D), lambda i: (i, 0)),
                   pl.BlockSpec((TOP_K, tm), lambda i: (0, i)),
                   pl.BlockSpec((TOP_K, tm), lambda i: (0, i)),
                   pl.BlockSpec((1, E, LANES), lambda i: (i, 0, 0))],
        out_shape=[jax.ShapeDtypeStruct((T, D), BF16),
                   jax.ShapeDtypeStruct((TOP_K, T), jnp.int32),
                   jax.ShapeDtypeStruct((TOP_K, T), F32),
                   jax.ShapeDtypeStruct((T // tm, E, LANES), jnp.int32)],
        compiler_params=_cparams(("arbitrary",)),
        name="router",
    )(x1, g, sc, sh, wr_hi, wr_lo, b_router, before, lower)


def _segment_pieces(tile, seg_start_ref, seg_dst_ref, seg_rows_ref, fn):
    def body(e, carry):
        idx = tile * N_EXPERTS + e
        local, dst, rows = seg_start_ref[idx], seg_dst_ref[idx], seg_rows_ref[idx]

        def pieces(sizes):
            for size in sizes:
                @pl.when(rows & size != 0)
                def _():
                    done = rows & (-2 * size)
                    fn(pl.multiple_of(local + done, SEG_ALIGN), pl.multiple_of(dst + done, SEG_ALIGN), size)

        pieces([s for s in SEG_PIECES if s < SEG_RARE])
        pl.when(rows >= SEG_RARE)(lambda: pieces([s for s in SEG_PIECES if s >= SEG_RARE]))
        return carry

    lax.fori_loop(0, N_EXPERTS, body, 0)


def _wait_pieces(tile, piece_count_ref, wait_one):
    for j, size in enumerate(SEG_PIECES):
        def body(_, carry, size=size):
            wait_one(size)
            return carry

        lax.fori_loop(0, piece_count_ref[tile * len(SEG_PIECES) + j], body, 0)


def _dispatch_kernel(seg_start_ref, seg_dst_ref, seg_rows_ref, piece_count_ref, h_ref, slot_ref, xs_in_ref, xs_ref,
                     xloc, sem, *, chunk):
    del xs_in_ref
    tile = pl.program_id(0)
    buf = tile % 2
    slots = slot_ref[...]
    h = h_ref[...]
    tm = h.shape[0]
    for r0 in range(0, xloc.shape[1], chunk):
        s = lax.broadcasted_iota(jnp.int32, (chunk, tm), 0) + r0
        onehot = jnp.zeros((chunk, tm), F32)
        for k in range(TOP_K):
            onehot = jnp.where(s == slots[k:k + 1, :], 1.0, onehot)
        xloc[buf, r0:r0 + chunk, :] = _dot(onehot.astype(BF16), h).astype(BF16)

    def copy(b, local, dst, rows):
        return pltpu.make_async_copy(xloc.at[b, pl.ds(local, rows)], xs_ref.at[pl.ds(dst, rows)], sem.at[b])

    _segment_pieces(tile, seg_start_ref, seg_dst_ref, seg_rows_ref, lambda *a: copy(buf, *a).start())

    @pl.when(tile > 0)
    def _():
        _wait_pieces(tile - 1, piece_count_ref, lambda rows: copy(1 - buf, 0, 0, rows).wait())

    @pl.when(tile == pl.num_programs(0) - 1)
    def _():
        _wait_pieces(tile, piece_count_ref, lambda rows: copy(buf, 0, 0, rows).wait())


def _dispatch(seg_start, seg_dst, seg_rows, piece_count, h2, slots, xs_buffer):
    T, D = h2.shape
    tm = MOE_TILE
    return pl.pallas_call(
        functools.partial(_dispatch_kernel, chunk=512),
        grid_spec=pltpu.PrefetchScalarGridSpec(
            num_scalar_prefetch=4,
            grid=(T // tm,),
            in_specs=[pl.BlockSpec((tm, D), lambda i, *_: (i, 0)),
                      pl.BlockSpec((TOP_K, tm), lambda i, *_: (0, i)),
                      pl.BlockSpec(memory_space=pl.ANY)],
            out_specs=pl.BlockSpec(memory_space=pl.ANY),
            scratch_shapes=[pltpu.VMEM((2, LOCAL_SLOTS, D), BF16), pltpu.SemaphoreType.DMA((2,))]),
        out_shape=jax.ShapeDtypeStruct(xs_buffer.shape, BF16),
        input_output_aliases={6: 0},
        compiler_params=_cparams(("arbitrary",)),
        name="moe_dispatch",
    )(seg_start, seg_dst, seg_rows, piece_count, h2, slots, xs_buffer)


def _expert_kernel(blk_expert_ref, blk_rows_ref, x_ref, w1_ref, w3_ref, w2_ref, y_ref):
    del blk_expert_ref
    rows = blk_rows_ref[pl.program_id(0)]

    @pl.when(rows > 0)
    def _():
        x = x_ref[...]
        a = _dot(x, w1_ref[0].astype(BF16))
        hid = a * jax.nn.sigmoid(a) * _dot(x, w3_ref[0].astype(BF16))
        y = _dot(hid.astype(BF16), w2_ref[0].astype(BF16))
        r = lax.broadcasted_iota(jnp.int32, y.shape, 0)
        y_ref[...] = jnp.where(r < rows, y, 0.0).astype(BF16)

    @pl.when(rows == 0)
    def _():
        y_ref[...] = jnp.zeros_like(y_ref)


def _experts(blk_expert, blk_rows, xs, w1, w3, w2, layer):
    n_rows, D = xs.shape
    F = EXPERT_FF
    R = EXPERT_BLOCK_ROWS
    return pl.pallas_call(
        _expert_kernel,
        grid_spec=pltpu.PrefetchScalarGridSpec(
            num_scalar_prefetch=2,
            grid=(n_rows // R,),
            in_specs=[pl.BlockSpec((R, D), lambda b, be, br: (b, 0)),
                      pl.BlockSpec((None, 1, D, F), lambda b, be, br: (layer, be[b], 0, 0)),
                      pl.BlockSpec((None, 1, D, F), lambda b, be, br: (layer, be[b], 0, 0)),
                      pl.BlockSpec((None, 1, F, D), lambda b, be, br: (layer, be[b], 0, 0))],
            out_specs=pl.BlockSpec((R, D), lambda b, be, br: (b, 0))),
        out_shape=jax.ShapeDtypeStruct((n_rows, D), BF16),
        compiler_params=_cparams(("arbitrary",)),
        name="moe_experts",
    )(blk_expert, blk_rows, xs, w1, w3, w2)


def _combine_kernel(seg_start_ref, seg_dst_ref, seg_rows_ref, piece_count_ref, ys_ref, slot_ref, gate_ref, h_ref,
                    x_ref, gf_ref, ws1_ref, ws3_ref, ws2_ref, o_ref, yloc, sem, *, chunk):
    tile = pl.program_id(0)
    buf = tile % 2

    def copy(b, local, dst, rows):
        return pltpu.make_async_copy(ys_ref.at[pl.ds(dst, rows)], yloc.at[b, pl.ds(local, rows)], sem.at[b])

    def fetch(t, b):
        yloc[b] = jnp.zeros(yloc.shape[1:], BF16)
        _segment_pieces(t, seg_start_ref, seg_dst_ref, seg_rows_ref, lambda *a: copy(b, *a).start())

    pl.when(tile == 0)(lambda: fetch(tile, buf))
    pl.when(tile + 1 < pl.num_programs(0))(lambda: fetch(tile + 1, 1 - buf))
    h = h_ref[...]
    a = _dot(h, ws1_ref[...])
    total = _dot((a * jax.nn.sigmoid(a) * _dot(h, ws3_ref[...])).astype(BF16), ws2_ref[...])
    _wait_pieces(tile, piece_count_ref, lambda rows: copy(buf, 0, 0, rows).wait())

    slots = slot_ref[...]
    gates = gate_ref[...]
    tm = h.shape[0]
    for c0 in range(0, yloc.shape[1], chunk):
        s = lax.broadcasted_iota(jnp.int32, (tm, chunk), 1) + c0
        weights = jnp.zeros((tm, chunk), F32)
        for k in range(TOP_K):
            weights = jnp.where(s == slots[:, k:k + 1], gates[:, k:k + 1], weights)
        total = total + _dot(weights.astype(BF16), yloc[buf, c0:c0 + chunk, :])
    o_ref[...] = x_ref[...] + gf_ref[0] * total


def _combine(seg_start, seg_dst, seg_rows, piece_count, ys, slots_tok, gates_tok, h2, x1, g_f, ws1, ws3, ws2, S):
    T, D = x1.shape
    tm = MOE_TILE
    per_batch = S // tm
    F = ws1.shape[1]
    return pl.pallas_call(
        functools.partial(_combine_kernel, chunk=512),
        grid_spec=pltpu.PrefetchScalarGridSpec(
            num_scalar_prefetch=4,
            grid=(T // tm,),
            in_specs=[pl.BlockSpec(memory_space=pl.ANY),
                      pl.BlockSpec((tm, TOP_K), lambda i, *_: (i, 0)),
                      pl.BlockSpec((tm, TOP_K), lambda i, *_: (i, 0)),
                      pl.BlockSpec((tm, D), lambda i, *_: (i, 0)),
                      pl.BlockSpec((tm, D), lambda i, *_: (i, 0)),
                      pl.BlockSpec((1, 1, D), lambda i, *_: (i // per_batch, 0, 0)),
                      pl.BlockSpec((D, F), lambda i, *_: (0, 0)),
                      pl.BlockSpec((D, F), lambda i, *_: (0, 0)),
                      pl.BlockSpec((F, D), lambda i, *_: (0, 0))],
            out_specs=pl.BlockSpec((tm, D), lambda i, *_: (i, 0)),
            scratch_shapes=[pltpu.VMEM((2, LOCAL_SLOTS, D), BF16), pltpu.SemaphoreType.DMA((2,))]),
        out_shape=jax.ShapeDtypeStruct((T, D), F32),
        compiler_params=_cparams(("arbitrary",)),
        name="moe_combine",
    )(seg_start, seg_dst, seg_rows, piece_count, ys, slots_tok, gates_tok, h2, x1, g_f, ws1, ws3, ws2)


def _moe_buffer_rows(T):
    R = EXPERT_BLOCK_ROWS
    n_segments = (T // MOE_TILE) * N_EXPERTS
    return -(-(T * TOP_K + n_segments * (SEG_ALIGN - 1) + N_EXPERTS * (R - SEG_ALIGN)) // R) * R


def _moe(x1, norm_g, sc_f, sh_f, g_f, w_router, b_router, w1, w3, w2, layer, ws1, ws3, ws2, S, xs_buffer):
    T, D = x1.shape
    E = N_EXPERTS
    R = EXPERT_BLOCK_ROWS
    wr_t = w_router.T
    wr_hi = wr_t.astype(BF16)
    wr_lo = (wr_t - wr_hi.astype(F32)).astype(BF16)
    h2, slots, gates, counts = _router(x1, norm_g.reshape(1, D), sc_f, sh_f, wr_hi, wr_lo, b_router.reshape(E, 1), S)

    seg_rows = (counts[:, :, 0] + (SEG_ALIGN - 1)) // SEG_ALIGN * SEG_ALIGN
    seg_start = jnp.cumsum(seg_rows, axis=1) - seg_rows
    expert_rows = jnp.sum(seg_rows, axis=0)
    region = (expert_rows + (R - 1)) // R * R
    region_end = jnp.cumsum(region)
    region_start = region_end - region
    seg_dst = region_start[None, :] + jnp.cumsum(seg_rows, axis=0) - seg_rows
    blk_first = jnp.arange(xs_buffer.shape[0] // R, dtype=jnp.int32) * R
    blk_expert = jnp.minimum(jnp.sum(blk_first[:, None] >= region_end[None, :], axis=1), E - 1).astype(jnp.int32)
    blk_rows = jnp.clip(expert_rows[blk_expert] - (blk_first - region_start[blk_expert]), 0, R).astype(jnp.int32)

    piece_count = jnp.stack([jnp.sum((seg_rows & size) != 0, axis=1) for size in SEG_PIECES], axis=1)

    flat = lambda a: a.reshape(-1).astype(jnp.int32)
    tables = (flat(seg_start), flat(seg_dst), flat(seg_rows), flat(piece_count))
    xs = _dispatch(*tables, h2, slots, xs_buffer)
    ys = _experts(blk_expert, blk_rows, xs, w1, w3, w2, layer)
    out = _combine(*tables, ys, slots.T, gates.T, h2, x1, g_f,
                   ws1.astype(BF16), ws3.astype(BF16), ws2.astype(BF16), S)
    return out, xs


def kernel(x, c, positions, w_mod, b_mod, norm_mix_g, norm_ffn_g, w_in, qn_g, kn_g, lam_q1, lam_k1, lam_q2, lam_k2, subln_g, w_proj_a, w_dw, b_dw, conv_ln_g, conv_ln_b, w_proj_b, b_proj_b, w_proj_c, w_out, w_router, b_router, w1, w3, w2, ws1, ws3, ws2):
    B, S, D = x.shape
    T = B * S
    depth = w_mod.shape[0]
    xt = x.reshape(T, D)

    inv = ROPE_THETA ** (-jnp.arange(0, DA_HEAD_DIM, 2, dtype=F32) / DA_HEAD_DIM)
    inv_lanes = jnp.tile(inv, LANES // (DA_HEAD_DIM // 2)).reshape(1, LANES)
    pos_lanes = jnp.broadcast_to(positions.astype(F32).reshape(T, 1), (T, LANES))
    cos, sin = _rope_tables(pos_lanes, inv_lanes)

    mod = _modulation(c, w_mod, b_mod)

    seg = jnp.arange(DA_WIDTH, dtype=jnp.int32) // DA_HEAD_DIM
    bd = jnp.where(seg[:, None] == seg[None, :], 1.0 / DA_HEAD_DIM, 0.0).astype(BF16)
    kk = jnp.arange(SB_BLOCK, dtype=jnp.int32)
    u = jnp.where(kk[:, None] >= kk[None, :], -1.0, 0.0).astype(BF16)
    u = jnp.concatenate([u, u], axis=0)
    n_seg = DA_WIDTH // DA_HEAD_DIM

    xs_buffer = jnp.zeros((_moe_buffer_rows(T), D), BF16)
    for l in range(depth):
        lambda_init = 0.8 - 0.6 * math.exp(-0.3 * l)
        sh_m, sc_m, g_m, sh_f, sc_f, g_f = [m.reshape(B, 1, D) for m in jnp.split(mod[l], 6, axis=-1)]

        proj = _norm_proj(xt, norm_mix_g[l].reshape(1, D), sc_m, sh_m, w_in[l].astype(BF16), S)

        q_r, k_r = _qk_prep(proj, cos, sin, bd,
                            jnp.tile(qn_g[l], n_seg).reshape(1, DA_WIDTH),
                            jnp.tile(kn_g[l], n_seg).reshape(1, DA_WIDTH))
        lam = (jnp.exp(jnp.sum(lam_q1[l] * lam_k1[l])) - jnp.exp(jnp.sum(lam_q2[l] * lam_k2[l])) + lambda_init)
        score_bound = (DA_HEAD_DIM ** 0.5) * jnp.max(jnp.abs(qn_g[l])) * jnp.max(jnp.abs(kn_g[l]))
        attn = functools.partial(_diff_attn, B=B, S=S, out_scale=1.0 - lambda_init)
        o_a = lax.cond(score_bound <= MAX_FIXED_SHIFT,
                       functools.partial(attn, online=False), functools.partial(attn, online=True),
                       score_bound.reshape(1, 1).astype(F32), jnp.full((1, LANES), lam, F32), q_r, k_r, proj,
                       subln_g[l].reshape(1, LANES))
        cb = _conv_module(proj, w_dw[l], b_dw[l].reshape(1, CONV_CH), conv_ln_g[l].reshape(1, CONV_CH),
                          conv_ln_b[l].reshape(1, CONV_CH), S)
        o_c = _sb_attn(proj, u, B, S)
        x1 = _merge(o_a, cb, o_c, proj, xt, g_m, w_proj_a[l].astype(BF16), w_proj_b[l].astype(BF16),
                    b_proj_b[l].reshape(1, D), w_proj_c[l].astype(BF16), w_out[l].astype(BF16), S)

        xt, xs_buffer = _moe(x1, norm_ffn_g[l], sc_f, sh_f, g_f, w_router[l], b_router[l],
                             w1, w3, w2, l, ws1[l], ws3[l], ws2[l], S, xs_buffer)

    return xt.reshape(B, S, D)
```
